```python
import jax, jax.numpy as jnp
from jax import lax
import numpy as np

D_MODEL = 1024
BATCH = 2
SEQ = 8192
DEPTH = 4
DEC_BATCH = 32
DEC_SEQ = 32
PAST_LEN = 2048

CHUNK = 64
HEAD_DIM = 64
A_HEADS = 4
A_WIDTH = A_HEADS * HEAD_DIM
CONV_W = 4
LRU_C = 8.0
B_HEADS = 4
B_WIDTH = B_HEADS * HEAD_DIM
C_HEADS = 8
C_KV_HEADS = 2
C_GROUP = C_HEADS // C_KV_HEADS
C_WIDTH = C_HEADS * HEAD_DIM
C_KV_WIDTH = C_KV_HEADS * HEAD_DIM
WINDOW = 128
D_MIX = A_WIDTH + B_WIDTH + C_WIDTH
IN_SIZES = (A_WIDTH, A_WIDTH, B_WIDTH, B_WIDTH, B_WIDTH, B_WIDTH, C_WIDTH, C_KV_WIDTH, C_KV_WIDTH)
D_IN = sum(IN_SIZES)
D_FF = -(-8 * D_MODEL // (3 * 256)) * 256
ALPHA = (2.0 * DEPTH) ** 0.25
BETA = (8.0 * DEPTH) ** -0.25
LN_EPS = 1e-5
NEG = -1e30

kernel_name = 'hybrid_stream_rglru_retention_swa_step'


def layer_norm(x, g, b):
    xf = x.astype(jnp.float32)
    mu = jnp.mean(xf, axis=-1, keepdims=True)
    var = jnp.mean(jnp.square(xf - mu), axis=-1, keepdims=True)
    y = (xf - mu) * lax.rsqrt(var + LN_EPS) * g.astype(jnp.float32) + b.astype(jnp.float32)
    return y.astype(x.dtype)


def causal_conv(u, buf, w, b):
    T = u.shape[1]
    up = jnp.concatenate([buf.astype(u.dtype), u], axis=1)
    y = b + up[:, 0:T] * w[0]
    for j in range(1, CONV_W):
        y = y + up[:, j:j + T] * w[j]
    return y, up[:, T:]


def rg_lru(u, h0, w_a, b_a, w_x, b_x, lam):
    nb, T, _ = u.shape
    ub = u.reshape(nb, T, A_HEADS, HEAD_DIM)
    r = jax.nn.sigmoid(jnp.einsum('bthi,hij->bthj', ub, w_a).reshape(nb, T, A_WIDTH) + b_a)
    gi = jax.nn.sigmoid(jnp.einsum('bthi,hij->bthj', ub, w_x).reshape(nb, T, A_WIDTH) + b_x)
    log_a = -LRU_C * r.astype(jnp.float32) * jax.nn.softplus(-lam.astype(jnp.float32))
    a = jnp.exp(log_a)
    bt = jnp.sqrt(-jnp.expm1(2.0 * log_a)) * (gi * u).astype(jnp.float32)
    bt = bt.at[:, 0].add(a[:, 0] * h0.astype(jnp.float32))

    def combine(left, right):
        a_l, b_l = left
        a_r, b_r = right
        return a_l * a_r, a_r * b_l + b_r

    _, h = lax.associative_scan(combine, (a, bt), axis=1)
    return h.astype(u.dtype), h[:, -1].astype(u.dtype)


def retention_log_gamma():
    return jnp.log1p(-2.0 ** (-5.0 - jnp.arange(B_HEADS, dtype=jnp.float32)))


def retention(q, k, v, s0, chunk):
    nb, T, H, Dh = q.shape
    nc = T // chunk
    lg = retention_log_gamma()
    qc = q.reshape(nb, nc, chunk, H, Dh).astype(jnp.float32)
    kc = k.reshape(nb, nc, chunk, H, Dh).astype(jnp.float32) * Dh ** -0.5
    vc = v.reshape(nb, nc, chunk, H, Dh).astype(jnp.float32)
    idx = jnp.arange(chunk, dtype=jnp.float32)
    diff = idx[:, None] - idx[None, :]
    dmat = jnp.where(diff >= 0, jnp.exp(jnp.maximum(diff, 0.0)[None] * lg[:, None, None]), 0.0)
    inner = jnp.einsum('bnihd,bnjhd->bnhij', qc, kc) * dmat[None, None]
    o_inner = jnp.einsum('bnhij,bnjhe->bnihe', inner, vc)
    w_end = jnp.exp((chunk - 1.0 - idx)[None, :] * lg[:, None])
    u_chunk = jnp.einsum('bnjhd,hj,bnjhe->bnhde', kc, w_end, vc)
    g_chunk = jnp.exp(chunk * lg)[None, :, None, None]

    def step(s, u_n):
        return g_chunk * s + u_n, s

    s_last, s_prev = lax.scan(step, s0.astype(jnp.float32), jnp.moveaxis(u_chunk, 1, 0))
    s_prev = jnp.moveaxis(s_prev, 0, 1)
    w_start = jnp.exp((idx + 1.0)[None, :] * lg[:, None])
    o_cross = jnp.einsum('bnihd,hi,bnhde->bnihe', qc, w_start, s_prev)
    return (o_inner + o_cross).reshape(nb, T, H, Dh), s_last


def head_norm(o, g, b):
    nb, T, H, Dh = o.shape
    mu = jnp.mean(o, axis=-1, keepdims=True)
    var = jnp.mean(jnp.square(o - mu), axis=-1, keepdims=True)
    on = ((o - mu) * lax.rsqrt(var + LN_EPS)).reshape(nb, T, H * Dh)
    return on * g.astype(jnp.float32) + b.astype(jnp.float32)


def alibi_slopes():
    return 2.0 ** (-8.0 * jnp.arange(1, C_HEADS + 1, dtype=jnp.float32) / C_HEADS)


def sink_softmax(s, sink):
    m = jnp.maximum(jnp.max(s, axis=-1, keepdims=True), sink)
    e = jnp.exp(s - m)
    return e / (jnp.sum(e, axis=-1, keepdims=True) + jnp.exp(sink - m))


def swa_prompt(q, k, v, sinks):
    nb, T = q.shape[0], q.shape[1]
    nc = T // CHUNK
    nw = WINDOW // CHUNK
    kw_len = (nw + 1) * CHUNK
    qc = q.reshape(nb, nc, CHUNK, C_KV_HEADS, C_GROUP, HEAD_DIM)
    pad = ((0, 0), (nw, 0), (0, 0), (0, 0), (0, 0))
    kp = jnp.pad(k.reshape(nb, nc, CHUNK, C_KV_HEADS, HEAD_DIM), pad)
    vp = jnp.pad(v.reshape(nb, nc, CHUNK, C_KV_HEADS, HEAD_DIM), pad)
    kw = jnp.concatenate([kp[:, j:j + nc] for j in range(nw + 1)], axis=2)
    vw = jnp.concatenate([vp[:, j:j + nc] for j in range(nw + 1)], axis=2)
    s = jnp.einsum('bnqhgd,bnkhd->bnhgqk', qc, kw).astype(jnp.float32) * HEAD_DIM ** -0.5
    qpos = jnp.arange(nc)[:, None] * CHUNK + jnp.arange(CHUNK)[None, :]
    kpos = (jnp.arange(nc)[:, None] - nw) * CHUNK + jnp.arange(kw_len)[None, :]
    dist = jnp.abs(qpos[:, :, None] - kpos[:, None, :]).astype(jnp.float32)
    slopes = alibi_slopes().reshape(C_KV_HEADS, C_GROUP)
    s = s - slopes[None, None, :, :, None, None] * dist[None, :, None, None]
    s = jnp.where((kpos >= 0)[None, :, None, None, None, :], s, NEG)
    sk = sinks.astype(jnp.float32).reshape(C_KV_HEADS, C_GROUP)[None, None, :, :, None, None]
    pr = sink_softmax(s, sk)
    o = jnp.einsum('bnhgqk,bnkhd->bnqhgd', pr.astype(v.dtype), vw)
    return o.reshape(nb, T, C_WIDTH)


def swa_sample(q, k, v, k_past, v_past, sinks):
    nb, T = q.shape[0], q.shape[1]
    wc = k_past.shape[1]
    kk = jnp.concatenate([k_past.astype(k.dtype), k], axis=1)
    vv = jnp.concatenate([v_past.astype(v.dtype), v], axis=1)
    qg = q.reshape(nb, T, C_KV_HEADS, C_GROUP, HEAD_DIM)
    s = jnp.einsum('bqhgd,bkhd->bhgqk', qg, kk).astype(jnp.float32) * HEAD_DIM ** -0.5
    dist = jnp.abs((wc + jnp.arange(T))[:, None] - jnp.arange(wc + T)[None, :]).astype(jnp.float32)
    slopes = alibi_slopes().reshape(C_KV_HEADS, C_GROUP)
    s = s - slopes[None, :, :, None, None] * dist
    sk = sinks.astype(jnp.float32).reshape(C_KV_HEADS, C_GROUP)[None, :, :, None, None]
    pr = sink_softmax(s, sk)
    o = jnp.einsum('bhgqk,bkhd->bqhgd', pr.astype(v.dtype), vv)
    return o.reshape(nb, T, C_WIDTH)


def mixer(h, conv_buf, lru_h, ret_s, k_past, v_past, p, l):
    nb, T, _ = h.shape
    z = jnp.einsum('btd,de->bte', h, p['w_in'][l])
    split_at = np.cumsum(IN_SIZES)[:-1].tolist()
    ax, ag, bq, bk, bv, bg, cq, ck, cv = jnp.split(z, split_at, axis=-1)

    def heads(t, n):
        return t.reshape(nb, T, n, HEAD_DIM)

    u, conv_new = causal_conv(ax, conv_buf, p['conv_w'][l], p['conv_b'][l])
    ha, lru_new = rg_lru(u, lru_h, p['w_rg_a'][l], p['b_rg_a'][l], p['w_rg_x'][l], p['b_rg_x'][l], p['lru_lambda'][l])
    o_a = ha * jax.nn.gelu(ag)
    ro, ret_new = retention(heads(bq, B_HEADS), heads(bk, B_HEADS), heads(bv, B_HEADS), ret_s, min(CHUNK, T))
    o_b = head_norm(ro, p['ret_gn_g'][l], p['ret_gn_b'][l]).astype(h.dtype) * jax.nn.silu(bg)
    q, k, v = heads(cq, C_HEADS), heads(ck, C_KV_HEADS), heads(cv, C_KV_HEADS)
    if k_past is None:
        o_c = swa_prompt(q, k, v, p['sinks'][l])
        k_new, v_new = k[:, -WINDOW:], v[:, -WINDOW:]
    else:
        o_c = swa_sample(q, k, v, k_past, v_past, p['sinks'][l])
        k_new, v_new = k, v
    o = jnp.concatenate([o_a, o_b, o_c.astype(h.dtype)], axis=-1)
    y = jnp.einsum('bte,ed->btd', o, p['w_out'][l]).astype(h.dtype)
    return y, (conv_new, lru_new, ret_new.astype(h.dtype), k_new, v_new)


def swiglu(h, w_gate, w_up, w_down):
    a = jnp.einsum('btd,df->btf', h, w_gate)
    b = jnp.einsum('btd,df->btf', h, w_up)
    return jnp.einsum('btf,fd->btd', jax.nn.silu(a) * b, w_down).astype(h.dtype)


def trunk(x, st_conv, st_lru, st_ret, c_k, c_v, p):
    nb = x.shape[0]
    h = layer_norm(x, p['ln_in_g'], p['ln_in_b'])
    new = ([], [], [], [], [])
    for l in range(DEPTH):
        if c_k is None:
            conv_buf = jnp.zeros((nb, CONV_W - 1, A_WIDTH), h.dtype)
            lru_h = jnp.zeros((nb, A_WIDTH), h.dtype)
            ret_s = jnp.zeros((nb, B_HEADS, HEAD_DIM, HEAD_DIM), jnp.float32)
            k_past, v_past = None, None
        else:
            conv_buf, lru_h, ret_s, k_past, v_past = st_conv[l], st_lru[l], st_ret[l], c_k[l], c_v[l]
        m, states = mixer(h, conv_buf, lru_h, ret_s, k_past, v_past, p, l)
        h = layer_norm(ALPHA * h + m, p['ln1_g'][l], p['ln1_b'][l])
        f = swiglu(h, p['w_gate'][l], p['w_up'][l], p['w_down'][l])
        h = layer_norm(ALPHA * h + f, p['ln2_g'][l], p['ln2_b'][l])
        for lst, s in zip(new, states):
            lst.append(s)
    return h, tuple(jnp.stack(s) for s in new)


def setup_inputs(seed: int = 0) -> dict:
    key = jax.random.key(seed)
    ks = jax.random.split(key, 32)
    f32 = jnp.float32

    def nrm(k, shape, scale):
        return jax.random.normal(k, shape, f32) * scale

    wc = min(WINDOW, PAST_LEN)
    a_c = jax.random.uniform(ks[0], (DEPTH, A_WIDTH), f32, 0.9, 0.999)
    sig = a_c ** (1.0 / LRU_C)
    lru_lambda = jnp.log(sig) - jnp.log1p(-sig)
    return {
        'x_prompt': nrm(ks[1], (BATCH, SEQ, D_MODEL), 1.0),
        'x_sample': nrm(ks[2], (DEC_BATCH, DEC_SEQ, D_MODEL), 1.0),
        'state_conv': nrm(ks[3], (DEPTH, DEC_BATCH, CONV_W - 1, A_WIDTH), 1.0),
        'state_lru': nrm(ks[4], (DEPTH, DEC_BATCH, A_WIDTH), 0.5),
        'state_ret': nrm(ks[5], (DEPTH, DEC_BATCH, B_HEADS, HEAD_DIM, HEAD_DIM), 0.3),
        'cache_k': nrm(ks[6], (DEPTH, DEC_BATCH, wc, C_KV_HEADS, HEAD_DIM), 1.0),
        'cache_v': nrm(ks[7], (DEPTH, DEC_BATCH, wc, C_KV_HEADS, HEAD_DIM), 1.0),
        'ln_in_g': 1.0 + nrm(ks[8], (D_MODEL,), 0.02),
        'ln_in_b': nrm(ks[9], (D_MODEL,), 0.02),
        'w_in': nrm(ks[10], (DEPTH, D_MODEL, D_IN), D_MODEL ** -0.5),
        'conv_w': nrm(ks[11], (DEPTH, CONV_W, A_WIDTH), CONV_W ** -0.5),
        'conv_b': nrm(ks[12], (DEPTH, A_WIDTH), 0.02),
        'w_rg_a': nrm(ks[13], (DEPTH, A_HEADS, HEAD_DIM, HEAD_DIM), HEAD_DIM ** -0.5),
        'b_rg_a': nrm(ks[14], (DEPTH, A_WIDTH), 0.02),
        'w_rg_x': nrm(ks[15], (DEPTH, A_HEADS, HEAD_DIM, HEAD_DIM), HEAD_DIM ** -0.5),
        'b_rg_x': nrm(ks[16], (DEPTH, A_WIDTH), 0.02),
        'lru_lambda': lru_lambda,
        'ret_gn_g': 1.0 + nrm(ks[17], (DEPTH, B_WIDTH), 0.02),
        'ret_gn_b': nrm(ks[18], (DEPTH, B_WIDTH), 0.02),
        'sinks': nrm(ks[19], (DEPTH, C_HEADS), 0.5),
        'w_out': nrm(ks[20], (DEPTH, D_MIX, D_MODEL), D_MIX ** -0.5 * BETA),
        'ln1_g': 1.0 + nrm(ks[21], (DEPTH, D_MODEL), 0.02),
        'ln1_b': nrm(ks[22], (DEPTH, D_MODEL), 0.02),
        'w_gate': nrm(ks[23], (DEPTH, D_MODEL, D_FF), D_MODEL ** -0.5),
        'w_up': nrm(ks[24], (DEPTH, D_MODEL, D_FF), D_MODEL ** -0.5),
        'w_down': nrm(ks[25], (DEPTH, D_FF, D_MODEL), D_FF ** -0.5 * BETA),
        'ln2_g': 1.0 + nrm(ks[26], (DEPTH, D_MODEL), 0.02),
        'ln2_b': nrm(ks[27], (DEPTH, D_MODEL), 0.02),
    }


def reference(x_prompt, x_sample, state_conv, state_lru, state_ret, cache_k, cache_v,
              ln_in_g, ln_in_b, w_in, conv_w, conv_b, w_rg_a, b_rg_a, w_rg_x, b_rg_x, lru_lambda,
              ret_gn_g, ret_gn_b, sinks, w_out, ln1_g, ln1_b, w_gate, w_up, w_down, ln2_g, ln2_b):
    p = {'ln_in_g': ln_in_g, 'ln_in_b': ln_in_b, 'w_in': w_in, 'conv_w': conv_w, 'conv_b': conv_b,
         'w_rg_a': w_rg_a, 'b_rg_a': b_rg_a, 'w_rg_x': w_rg_x, 'b_rg_x': b_rg_x, 'lru_lambda': lru_lambda,
         'ret_gn_g': ret_gn_g, 'ret_gn_b': ret_gn_b, 'sinks': sinks, 'w_out': w_out,
         'ln1_g': ln1_g, 'ln1_b': ln1_b, 'w_gate': w_gate, 'w_up': w_up, 'w_down': w_down,
         'ln2_g': ln2_g, 'ln2_b': ln2_b}
    y_prompt, (p_conv, p_lru, p_ret, p_k, p_v) = trunk(x_prompt, None, None, None, None, None, p)
    y_sample, (s_conv, s_lru, s_ret, s_k, s_v) = trunk(x_sample, state_conv, state_lru, state_ret, cache_k, cache_v, p)
    return (y_prompt, y_sample, p_conv, p_lru, p_ret, p_k, p_v, s_conv, s_lru, s_ret, s_k, s_v)
```

```python
import functools
import math

import numpy as np
import jax
import jax.numpy as jnp
from jax import lax
from jax.experimental import pallas as pl
from jax.experimental.pallas import tpu as pltpu

F32 = jnp.float32
BF16 = jnp.bfloat16

D_MODEL = 1024
DEPTH = 4
HEAD_DIM = 64
A_WIDTH = 256
B_HEADS = 4
B_WIDTH = 256
C_HEADS = 8
C_KV_HEADS = 2
C_GROUP = 4
C_WIDTH = 512
C_KV_WIDTH = 128
CONV_W = 4
LRU_C = 8.0
WINDOW = 128
D_MIX = 1024
D_IN = 2304
D_FF = 2816
ALPHA = (2.0 * DEPTH) ** 0.25
LN_EPS = 1e-5
NEG = -1e30

OFF_AX, OFF_AG = 0, 256
OFF_BQ, OFF_BK, OFF_BV, OFF_BG = 512, 768, 1024, 1280
OFF_CQ, OFF_CK, OFF_CV = 1536, 2048, 2176

SUBLANES = 8
VMEM_LIMIT_BYTES = 52 * 1024 * 1024


def _ln(x, g, b):
    mu = jnp.mean(x, axis=-1, keepdims=True)
    d = x - mu
    var = jnp.mean(d * d, axis=-1, keepdims=True)
    return d * lax.rsqrt(var + LN_EPS) * g + b


def _dot(a, b):
    return jnp.dot(a, b, preferred_element_type=F32)


def _dot_nt(a, b):
    return lax.dot_general(a, b, (((1,), (1,)), ((), ())), preferred_element_type=F32)


def _dot_tn(a, b):
    return lax.dot_general(a, b, (((0,), (0,)), ((), ())), preferred_element_type=F32)


def _retention_tables(chunk):
    lg = np.log1p(-(2.0 ** (-5.0 - np.arange(B_HEADS, dtype=np.float64))))
    idx = np.arange(chunk, dtype=np.float64)
    diff = idx[:, None] - idx[None, :]
    dmat = np.where(diff >= 0, np.exp(np.maximum(diff, 0.0)[None] * lg[:, None, None]), 0.0)
    dmat = dmat * HEAD_DIM ** -0.5
    dmat_all = np.concatenate([dmat[h] for h in range(B_HEADS)], axis=1)
    w_start = np.exp((idx + 1.0)[:, None] * lg[None, :])
    w_end = np.exp((chunk - 1.0 - idx)[:, None] * lg[None, :]) * HEAD_DIM ** -0.5
    w_start = np.repeat(w_start, HEAD_DIM, axis=1)
    w_end = np.repeat(w_end, HEAD_DIM, axis=1)
    g_chunk = np.repeat(np.exp(chunk * lg), HEAD_DIM)
    head = np.arange(B_WIDTH) // HEAD_DIM
    blk = (head[:, None] == head[None, :]).astype(np.float64)
    g_bd = blk * g_chunk[:, None]
    avg_bd = blk / HEAD_DIM
    return (jnp.asarray(dmat_all, F32), jnp.asarray(w_start, F32), jnp.asarray(w_end, F32),
            jnp.asarray(g_bd, F32), jnp.asarray(blk, F32), jnp.asarray(avg_bd, BF16))


def _alibi_table(chunk):
    w = WINDOW + chunk
    slopes = 2.0 ** (-8.0 * np.arange(1, C_HEADS + 1, dtype=np.float64) / C_HEADS)
    dist = np.abs((WINDOW + np.arange(chunk))[:, None] - np.arange(w)[None, :]).astype(np.float64)
    tab = slopes[:, None, None] * dist[None]
    tab = tab.reshape(C_KV_HEADS, C_GROUP * chunk, w)
    return jnp.asarray(tab, F32)


def _mixer_body(cfg, *refs):
    G, TB, C, NT, has_past, ln_in = cfg
    R = G * TB
    NCH = TB // C
    W = WINDOW + C
    it = iter(refs)
    x_ref = next(it)
    if has_past:
        sconv_ref, slru_ref, sret_ref, ck_ref, cv_ref = (next(it) for _ in range(5))
    (ln_in_g, ln_in_b, w_in_ref, conv_w_ref, conv_b_ref, w_gate_ref, b_gate_ref, lam_ref,
     gn_g_ref, gn_b_ref, sink_ref, w_out_ref, ln1_g, ln1_b,
     dmat_ref, wstart_ref, wend_ref, gbd_ref, blk_ref, avg_ref, alibi_ref) = (next(it) for _ in range(21))
    h1_ref, oconv_ref, olru_ref, oret_ref, ok_ref, ov_ref = (next(it) for _ in range(6))
    z_ref, o_ref, sa_ref, sb_ref, axbuf, hc_ref, kwin, vwin, sbd = (next(it) for _ in range(9))

    t_idx = pl.program_id(1)

    @pl.when(t_idx == 0)
    def _init():
        for g in range(G):
            if has_past:
                axbuf[g, SUBLANES - 3:SUBLANES, :] = sconv_ref[g]
                hc_ref[g] = slru_ref[g]
                sbd[g] = jnp.zeros((B_WIDTH, B_WIDTH), F32)
                for h in range(B_HEADS):
                    sl = slice(h * HEAD_DIM, (h + 1) * HEAD_DIM)
                    sbd[g, sl, sl] = sret_ref[g, h]
                kwin[g, 0:WINDOW, :] = ck_ref[g].astype(BF16)
                vwin[g, 0:WINDOW, :] = cv_ref[g].astype(BF16)
            else:
                axbuf[g, 0:SUBLANES, :] = jnp.zeros((SUBLANES, A_WIDTH), F32)
                hc_ref[g] = jnp.zeros((1, A_WIDTH), F32)
                sbd[g] = jnp.zeros((B_WIDTH, B_WIDTH), F32)
                kwin[g, 0:WINDOW, :] = jnp.zeros((WINDOW, C_KV_WIDTH), BF16)
                vwin[g, 0:WINDOW, :] = jnp.zeros((WINDOW, C_KV_WIDTH), BF16)

    x = x_ref[...].reshape(R, D_MODEL)
    if ln_in:
        h = _ln(x, ln_in_g[...], ln_in_b[...])
    else:
        h = x
    z_ref[...] = _dot(h.astype(BF16), w_in_ref[...])

    lam = lam_ref[...]
    nl = -lam
    softplus_nl = jnp.maximum(nl, 0.0) + jnp.log1p(jnp.exp(-jnp.abs(nl)))
    cw = conv_w_ref[...]
    row8 = lax.broadcasted_iota(jnp.int32, (SUBLANES, A_WIDTH), 0)
    for g in range(G):
        rows = slice(g * TB, (g + 1) * TB)
        ax = z_ref[rows, OFF_AX:OFF_AX + A_WIDTH]
        axbuf[g, SUBLANES:SUBLANES + TB, :] = ax
        u = conv_b_ref[...] + axbuf[g, SUBLANES - 3:SUBLANES - 3 + TB, :] * cw[0:1, :]
        u = u + axbuf[g, SUBLANES - 2:SUBLANES - 2 + TB, :] * cw[1:2, :]
        u = u + axbuf[g, SUBLANES - 1:SUBLANES - 1 + TB, :] * cw[2:3, :]
        u = u + ax * cw[3:4, :]
        if NT > 1:
            axbuf[g, SUBLANES - 3:SUBLANES, :] = axbuf[g, SUBLANES + TB - 3:SUBLANES + TB, :]
        gz = _dot(u.astype(BF16), w_gate_ref[...]) + b_gate_ref[...]
        r = jax.nn.sigmoid(gz[:, 0:A_WIDTH])
        gi = jax.nn.sigmoid(gz[:, A_WIDTH:2 * A_WIDTH])
        log_a = (-LRU_C) * r * softplus_nl
        a = jnp.exp(log_a)
        th = jnp.tanh(log_a)
        one_minus_a2 = (-2.0) * th / (1.0 - th)
        bt = jnp.sqrt(one_minus_a2) * (gi * u)
        sa_ref[rows, :] = a
        sb_ref[rows, :] = bt

        def tile_step(i, hprev, g=g):
            r0 = pl.multiple_of(g * TB + i * SUBLANES, SUBLANES)
            ta = sa_ref[pl.ds(r0, SUBLANES), :]
            tb = sb_ref[pl.ds(r0, SUBLANES), :]
            for s in (1, 2, 4):
                keep = row8 >= s
                tb = jnp.where(keep, ta * pltpu.roll(tb, s, 0) + tb, tb)
                ta = jnp.where(keep, ta * pltpu.roll(ta, s, 0), ta)
            ht = ta * hprev + tb
            sb_ref[pl.ds(r0, SUBLANES), :] = ht
            return jnp.broadcast_to(ht[SUBLANES - 1:SUBLANES, :], (SUBLANES, A_WIDTH))

        h0 = jnp.broadcast_to(hc_ref[g], (SUBLANES, A_WIDTH))
        hlast = lax.fori_loop(0, TB // SUBLANES, tile_step, h0, unroll=4)
        hc_ref[g] = hlast[0:1, :]
        ha = sb_ref[rows, :]
        ag = z_ref[rows, OFF_AG:OFF_AG + A_WIDTH]
        o_ref[rows, 0:A_WIDTH] = ha * jax.nn.gelu(ag)

    lane_head = lax.broadcasted_iota(jnp.int32, (C, B_WIDTH), 1) // HEAD_DIM
    dmat = dmat_ref[...]
    col = lax.broadcasted_iota(jnp.int32, (C_GROUP * C, W), 1)
    for g in range(G):
        rows_g = slice(g * TB, (g + 1) * TB)
        kwin[g, WINDOW:WINDOW + TB, :] = z_ref[rows_g, OFF_CK:OFF_CK + C_KV_WIDTH].astype(BF16)
        vwin[g, WINDOW:WINDOW + TB, :] = z_ref[rows_g, OFF_CV:OFF_CV + C_KV_WIDTH].astype(BF16)
        for c in range(NCH):
            rows = slice(g * TB + c * C, g * TB + (c + 1) * C)
            q = z_ref[rows, OFF_BQ:OFF_BQ + B_WIDTH]
            k = z_ref[rows, OFF_BK:OFF_BK + B_WIDTH]
            v = z_ref[rows, OFF_BV:OFF_BV + B_WIDTH]
            bgate = z_ref[rows, OFF_BG:OFF_BG + B_WIDTH]
            kx = jnp.concatenate([jnp.where(lane_head == hh, k, 0.0) for hh in range(B_HEADS)], axis=0).astype(BF16)
            vx = jnp.concatenate([jnp.where(lane_head == hh, v, 0.0) for hh in range(B_HEADS)], axis=0).astype(BF16)
            inner = _dot_nt(q.astype(BF16), kx) * dmat
            s_prev = sbd[g]
            ro = _dot(inner.astype(BF16), vx) + _dot((q * wstart_ref[...]).astype(BF16), s_prev.astype(BF16))
            upd = _dot_tn((k * wend_ref[...]).astype(BF16), v.astype(BF16))
            sbd[g] = gbd_ref[...] * s_prev + upd * blk_ref[...]
            mu = _dot(ro.astype(BF16), avg_ref[...])
            dlt = ro - mu
            var = _dot((dlt * dlt).astype(BF16), avg_ref[...])
            on = dlt * lax.rsqrt(var + LN_EPS) * gn_g_ref[...] + gn_b_ref[...]
            o_ref[rows, A_WIDTH:A_WIDTH + B_WIDTH] = on * (bgate * jax.nn.sigmoid(bgate))
            kw = kwin[g, c * C:c * C + W, :]
            vw = vwin[g, c * C:c * C + W, :]
            for kvh in range(C_KV_HEADS):
                qs = jnp.concatenate(
                    [z_ref[rows, OFF_CQ + (kvh * C_GROUP + j) * HEAD_DIM:OFF_CQ + (kvh * C_GROUP + j + 1) * HEAD_DIM]
                     for j in range(C_GROUP)], axis=0)
                s = _dot_nt(qs.astype(BF16), kw[:, kvh * HEAD_DIM:(kvh + 1) * HEAD_DIM]) * (HEAD_DIM ** -0.5)
                s = s - alibi_ref[kvh]
                if (not has_past) and c * C < WINDOW:
                    n_invalid = jnp.where(t_idx == 0, WINDOW - c * C, 0)
                    s = jnp.where(col >= n_invalid, s, NEG)
                sk = sink_ref[kvh]
                m = jnp.maximum(jnp.max(s, axis=-1, keepdims=True), sk)
                e = jnp.exp(s - m)
                den = jnp.sum(e, axis=-1, keepdims=True) + jnp.exp(sk - m)
                p = e * (1.0 / den)
                oc = _dot(p.astype(BF16), vw[:, kvh * HEAD_DIM:(kvh + 1) * HEAD_DIM])
                oc = jnp.concatenate([oc[j * C:(j + 1) * C, :] for j in range(C_GROUP)], axis=1)
                lo = A_WIDTH + B_WIDTH + kvh * C_GROUP * HEAD_DIM
                o_ref[rows, lo:lo + C_GROUP * HEAD_DIM] = oc
        if NT > 1:
            kwin[g, 0:WINDOW, :] = kwin[g, TB:TB + WINDOW, :]
            vwin[g, 0:WINDOW, :] = vwin[g, TB:TB + WINDOW, :]

    y = _dot(o_ref[...].astype(BF16), w_out_ref[...])
    h1 = _ln(ALPHA * h + y, ln1_g[...], ln1_b[...])
    h1_ref[...] = h1.reshape(G, TB, D_MODEL)

    @pl.when(t_idx == NT - 1)
    def _fin():
        kn = ok_ref.shape[1]
        for g in range(G):
            oconv_ref[g] = axbuf[g, SUBLANES + TB - 3:SUBLANES + TB, :]
            olru_ref[g] = hc_ref[g]
            for h in range(B_HEADS):
                sl = slice(h * HEAD_DIM, (h + 1) * HEAD_DIM)
                oret_ref[g, h] = sbd[g, sl, sl]
            last = slice((g + 1) * TB - kn, (g + 1) * TB)
            ok_ref[g] = z_ref[last, OFF_CK:OFF_CK + C_KV_WIDTH]
            ov_ref[g] = z_ref[last, OFF_CV:OFF_CV + C_KV_WIDTH]


def _const_spec(shape):
    nd = len(shape)
    return pl.BlockSpec(shape, lambda s, t, _nd=nd: (0,) * _nd, pipeline_mode=pl.Buffered(1))


def _mixer_call(x, states, lp, tabs, *, G, TB, C, ln_in):
    B, T, _ = x.shape
    NS, NT = B // G, T // TB
    has_past = states is not None
    KN = C if has_past else WINDOW
    cfg = (G, TB, C, NT, has_past, ln_in)

    in_specs = [pl.BlockSpec((G, TB, D_MODEL), lambda s, t: (s, t, 0))]
    args = [x]
    if has_past:
        for st in states:
            nd = st.ndim
            in_specs.append(pl.BlockSpec((G,) + st.shape[1:], lambda s, t, _nd=nd: (s,) + (0,) * (_nd - 1)))
            args.append(st)
    consts = list(lp) + list(tabs)
    for cst in consts:
        in_specs.append(_const_spec(cst.shape))
        args.append(cst)

    out_shape = (
        jax.ShapeDtypeStruct((B, T, D_MODEL), F32),
        jax.ShapeDtypeStruct((B, CONV_W - 1, A_WIDTH), F32),
        jax.ShapeDtypeStruct((B, 1, A_WIDTH), F32),
        jax.ShapeDtypeStruct((B, B_HEADS, HEAD_DIM, HEAD_DIM), F32),
        jax.ShapeDtypeStruct((B, KN, C_KV_WIDTH), F32),
        jax.ShapeDtypeStruct((B, KN, C_KV_WIDTH), F32),
    )
    out_specs = (
        pl.BlockSpec((G, TB, D_MODEL), lambda s, t: (s, t, 0)),
        pl.BlockSpec((G, CONV_W - 1, A_WIDTH), lambda s, t: (s, 0, 0)),
        pl.BlockSpec((G, 1, A_WIDTH), lambda s, t: (s, 0, 0)),
        pl.BlockSpec((G, B_HEADS, HEAD_DIM, HEAD_DIM), lambda s, t: (s, 0, 0, 0)),
        pl.BlockSpec((G, KN, C_KV_WIDTH), lambda s, t: (s, 0, 0)),
        pl.BlockSpec((G, KN, C_KV_WIDTH), lambda s, t: (s, 0, 0)),
    )
    R = G * TB
    scratch = [
        pltpu.VMEM((R, D_IN), F32),
        pltpu.VMEM((R, D_MIX), F32),
        pltpu.VMEM((R, A_WIDTH), F32),
        pltpu.VMEM((R, A_WIDTH), F32),
        pltpu.VMEM((G, SUBLANES + TB, A_WIDTH), F32),
        pltpu.VMEM((G, 1, A_WIDTH), F32),
        pltpu.VMEM((G, WINDOW + TB, C_KV_WIDTH), BF16),
        pltpu.VMEM((G, WINDOW + TB, C_KV_WIDTH), BF16),
        pltpu.VMEM((G, B_WIDTH, B_WIDTH), F32),
    ]
    return pl.pallas_call(
        functools.partial(_mixer_body, cfg),
        grid=(NS, NT),
        in_specs=in_specs,
        out_specs=out_specs,
        out_shape=out_shape,
        scratch_shapes=scratch,
        compiler_params=pltpu.CompilerParams(
            dimension_semantics=("arbitrary", "arbitrary"),
            vmem_limit_bytes=VMEM_LIMIT_BYTES),
        name="mixer_past" if has_past else "mixer_prompt",
    )(*args)


FFN_SPLIT = 2
FFN_CHUNK = D_FF // FFN_SPLIT


def _ffn_body(h_ref, wg_ref, wu_ref, wd_ref, g_ref, b_ref, out_ref):
    h = h_ref[...]
    hb = h.astype(BF16)
    acc = jnp.zeros(h.shape, F32)
    for f in range(FFN_SPLIT):
        cols = slice(f * FFN_CHUNK, (f + 1) * FFN_CHUNK)
        a = _dot(hb, wg_ref[:, cols])
        b = _dot(hb, wu_ref[:, cols])
        act = (a * jax.nn.sigmoid(a) * b).astype(BF16)
        acc = acc + _dot(act, wd_ref[cols, :])
    out_ref[...] = _ln(ALPHA * h + acc, g_ref[...], b_ref[...])


def _ffn_call(h2d, wg, wu, wd, g, b, *, TM):
    N = h2d.shape[0]
    cs = lambda shape: pl.BlockSpec(shape, lambda i: (0, 0), pipeline_mode=pl.Buffered(1))
    return pl.pallas_call(
        _ffn_body,
        grid=(N // TM,),
        in_specs=[pl.BlockSpec((TM, D_MODEL), lambda i: (i, 0)),
                  cs((D_MODEL, D_FF)), cs((D_MODEL, D_FF)), cs((D_FF, D_MODEL)),
                  cs((1, D_MODEL)), cs((1, D_MODEL))],
        out_specs=pl.BlockSpec((TM, D_MODEL), lambda i: (i, 0)),
        out_shape=jax.ShapeDtypeStruct((N, D_MODEL), F32),
        compiler_params=pltpu.CompilerParams(
            dimension_semantics=("arbitrary",),
            vmem_limit_bytes=VMEM_LIMIT_BYTES),
        name="ffn",
    )(h2d, wg, wu, wd, g, b)


def _block_diag(w):
    hh = w.shape[0]
    eye = jnp.eye(hh, dtype=w.dtype)
    return (eye[:, None, :, None] * w[:, :, None, :]).reshape(hh * HEAD_DIM, hh * HEAD_DIM)


def kernel(x_prompt, x_sample, state_conv, state_lru, state_ret, cache_k, cache_v, ln_in_g, ln_in_b, w_in, conv_w, conv_b, w_rg_a, b_rg_a, w_rg_x, b_rg_x, lru_lambda, ret_gn_g, ret_gn_b, sinks, w_out, ln1_g, ln1_b, w_gate, w_up, w_down, ln2_g, ln2_b):
    row = lambda v: v.reshape(1, -1)
    w_in_b = w_in.astype(BF16)
    w_out_b = w_out.astype(BF16)
    w_gate_b = w_gate.astype(BF16)
    w_up_b = w_up.astype(BF16)
    w_down_b = w_down.astype(BF16)

    PROMPT = dict(G=1, TB=512, C=64)
    SAMPLE = dict(G=8, TB=32, C=32)
    tabs_p = _retention_tables(PROMPT["C"]) + (_alibi_table(PROMPT["C"]),)
    tabs_s = _retention_tables(SAMPLE["C"]) + (_alibi_table(SAMPLE["C"]),)

    def layer_params(l, chunk):
        w_gates = jnp.concatenate([_block_diag(w_rg_a[l]), _block_diag(w_rg_x[l])], axis=1).astype(BF16)
        b_gates = jnp.concatenate([b_rg_a[l], b_rg_x[l]]).reshape(1, -1)
        sink_rows = jnp.repeat(sinks[l], chunk).reshape(C_KV_HEADS, C_GROUP * chunk, 1)
        return (row(ln_in_g), row(ln_in_b), w_in_b[l], conv_w[l], row(conv_b[l]), w_gates, b_gates,
                row(lru_lambda[l]), row(ret_gn_g[l]), row(ret_gn_b[l]), sink_rows, w_out_b[l],
                row(ln1_g[l]), row(ln1_b[l]))

    hp, hs = x_prompt, x_sample
    outs_p, outs_s = [], []
    nb_p, nb_s = x_prompt.shape[0], x_sample.shape[0]
    wc = cache_k.shape[2]
    for l in range(DEPTH):
        res = _mixer_call(hp, None, layer_params(l, PROMPT["C"]), tabs_p, ln_in=(l == 0), **PROMPT)
        hp, st = res[0], res[1:]
        outs_p.append(st)
        hp = _ffn_call(hp.reshape(-1, D_MODEL), w_gate_b[l], w_up_b[l], w_down_b[l], row(ln2_g[l]), row(ln2_b[l]),
                       TM=512).reshape(x_prompt.shape)
        states = (state_conv[l], state_lru[l].reshape(nb_s, 1, A_WIDTH), state_ret[l],
                  cache_k[l].reshape(nb_s, wc, C_KV_WIDTH), cache_v[l].reshape(nb_s, wc, C_KV_WIDTH))
        res = _mixer_call(hs, states, layer_params(l, SAMPLE["C"]), tabs_s, ln_in=(l == 0), **SAMPLE)
        hs, st = res[0], res[1:]
        outs_s.append(st)
        hs = _ffn_call(hs.reshape(-1, D_MODEL), w_gate_b[l], w_up_b[l], w_down_b[l], row(ln2_g[l]), row(ln2_b[l]),
                       TM=512).reshape(x_sample.shape)

    def stack(outs, nb, kn):
        conv = jnp.stack([o[0] for o in outs])
        lru = jnp.stack([o[1].reshape(nb, A_WIDTH) for o in outs])
        ret = jnp.stack([o[2] for o in outs])
        kk = jnp.stack([o[3].reshape(nb, kn, C_KV_HEADS, HEAD_DIM) for o in outs])
        vv = jnp.stack([o[4].reshape(nb, kn, C_KV_HEADS, HEAD_DIM) for o in outs])
        return conv, lru, ret, kk, vv

    p_conv, p_lru, p_ret, p_k, p_v = stack(outs_p, nb_p, WINDOW)
    s_conv, s_lru, s_ret, s_k, s_v = stack(outs_s, nb_s, SAMPLE["C"])
    return (hp, hs, p_conv, p_lru, p_ret, p_k, p_v, s_conv, s_lru, s_ret, s_k, s_v)
```

```python
import functools
import math

import numpy as np
import jax
import jax.numpy as jnp
from jax import lax
from jax.experimental import pallas as pl
from jax.experimental.pallas import tpu as pltpu

F32 = jnp.float32
BF16 = jnp.bfloat16

D_MODEL = 1024
DEPTH = 4
HEAD_DIM = 64
A_WIDTH = 256
B_HEADS = 4
B_WIDTH = 256
C_HEADS = 8
C_KV_HEADS = 2
C_GROUP = 4
C_WIDTH = 512
C_KV_WIDTH = 128
CONV_W = 4
LRU_C = 8.0
WINDOW = 128
ATTN_CHUNK = 64
D_MIX = 1024
D_IN = 2304
D_FF = 2816
ALPHA = (2.0 * DEPTH) ** 0.25
LN_EPS = 1e-5
NEG = -1e30

OFF_AX, OFF_AG = 0, 256
OFF_BQ, OFF_BK, OFF_BV, OFF_BG = 512, 768, 1024, 1280
OFF_CQ, OFF_CK, OFF_CV = 1536, 2048, 2176

SUBLANES = 8
VMEM_LIMIT_BYTES = 52 * 1024 * 1024


def _ln(x, g, b):
    mu = jnp.mean(x, axis=-1, keepdims=True)
    d = x - mu
    var = jnp.mean(d * d, axis=-1, keepdims=True)
    return d * lax.rsqrt(var + LN_EPS) * g + b


def _dot(a, b):
    return jnp.dot(a, b, preferred_element_type=F32)


def _dot_nt(a, b):
    return lax.dot_general(a, b, (((1,), (1,)), ((), ())), preferred_element_type=F32)


def _dot_tn(a, b):
    return lax.dot_general(a, b, (((0,), (0,)), ((), ())), preferred_element_type=F32)


def _retention_tables(chunk):
    lg = np.log1p(-(2.0 ** (-5.0 - np.arange(B_HEADS, dtype=np.float64))))
    idx = np.arange(chunk, dtype=np.float64)
    diff = idx[:, None] - idx[None, :]
    dmat = np.where(diff >= 0, np.exp(np.maximum(diff, 0.0)[None] * lg[:, None, None]), 0.0)
    dmat = dmat * HEAD_DIM ** -0.5
    dmat_all = np.concatenate([dmat[h] for h in range(B_HEADS)], axis=1)
    w_start = np.exp((idx + 1.0)[:, None] * lg[None, :])
    w_end = np.exp((chunk - 1.0 - idx)[:, None] * lg[None, :]) * HEAD_DIM ** -0.5
    w_start = np.repeat(w_start, HEAD_DIM, axis=1)
    w_end = np.repeat(w_end, HEAD_DIM, axis=1)
    g_chunk = np.repeat(np.exp(chunk * lg), HEAD_DIM)
    head = np.arange(B_WIDTH) // HEAD_DIM
    blk = (head[:, None] == head[None, :]).astype(np.float64)
    g_bd = blk * g_chunk[:, None]
    avg_bd = blk / HEAD_DIM
    return (jnp.asarray(dmat_all, F32), jnp.asarray(w_start, F32), jnp.asarray(w_end, F32),
            jnp.asarray(g_bd, F32), jnp.asarray(blk, F32), jnp.asarray(avg_bd, BF16))


def _alibi_table(qs):
    w = WINDOW + qs
    slopes = 2.0 ** (-8.0 * np.arange(1, C_HEADS + 1, dtype=np.float64) / C_HEADS)
    qpos = np.arange(qs)
    kpos = np.arange(w) - WINDOW
    dist = np.abs(qpos[:, None] - kpos[None, :]).astype(np.float64)
    q_chunk = qpos // ATTN_CHUNK
    k_chunk = np.floor_divide(kpos, ATTN_CHUNK)
    visible = (k_chunk[None, :] <= q_chunk[:, None]) & (k_chunk[None, :] >= q_chunk[:, None] - WINDOW // ATTN_CHUNK)
    tab = slopes[:, None, None] * dist[None]
    tab = np.where(visible[None], tab, -NEG)
    tab = tab.reshape(C_KV_HEADS, C_GROUP * qs, w).transpose(0, 2, 1)
    return jnp.asarray(tab, F32)


def _mixer_body(cfg, *refs):
    G, TB, CR, QS, NT, has_past, ln_in = cfg
    R = G * TB
    it = iter(refs)
    x_ref = next(it)
    if has_past:
        sconv_ref, slru_ref, sret_ref, ck_ref, cv_ref = (next(it) for _ in range(5))
    (ln_in_g, ln_in_b, w_in_ref, conv_w_ref, conv_b_ref, w_gate_ref, b_gate_ref, lam_ref,
     gn_g_ref, gn_b_ref, sink_ref, w_out_ref, ln1_g, ln1_b,
     dmat_ref, wstart_ref, wend_ref, gbd_ref, blk_ref, avg_ref, alibi_ref) = (next(it) for _ in range(21))
    h1_ref, oconv_ref, olru_ref, oret_ref, ok_ref, ov_ref = (next(it) for _ in range(6))
    z_ref, o_ref, sa_ref, sb_ref, axbuf, hc_ref, kwin, vwin, sbd = (next(it) for _ in range(9))

    t_idx = pl.program_id(1)

    @pl.when(t_idx == 0)
    def _init():
        for g in range(G):
            if has_past:
                axbuf[g, SUBLANES - 3:SUBLANES, :] = sconv_ref[g]
                hc_ref[g] = slru_ref[g]
                sbd[g] = jnp.zeros((B_WIDTH, B_WIDTH), F32)
                for h in range(B_HEADS):
                    sl = slice(h * HEAD_DIM, (h + 1) * HEAD_DIM)
                    sbd[g, sl, sl] = sret_ref[g, h]
                kwin[g, 0:WINDOW, :] = ck_ref[g].astype(BF16)
                vwin[g, 0:WINDOW, :] = cv_ref[g].astype(BF16)
            else:
                axbuf[g, 0:SUBLANES, :] = jnp.zeros((SUBLANES, A_WIDTH), F32)
                hc_ref[g] = jnp.zeros((1, A_WIDTH), F32)
                sbd[g] = jnp.zeros((B_WIDTH, B_WIDTH), F32)
                kwin[g, 0:WINDOW, :] = jnp.zeros((WINDOW, C_KV_WIDTH), BF16)
                vwin[g, 0:WINDOW, :] = jnp.zeros((WINDOW, C_KV_WIDTH), BF16)

    x = x_ref[...].reshape(R, D_MODEL)
    if ln_in:
        h = _ln(x, ln_in_g[...], ln_in_b[...])
    else:
        h = x
    z_ref[...] = _dot(h.astype(BF16), w_in_ref[...])

    lam = lam_ref[...]
    nl = -lam
    softplus_nl = jnp.maximum(nl, 0.0) + jnp.log1p(jnp.exp(-jnp.abs(nl)))
    cw = conv_w_ref[...]
    row8 = lax.broadcasted_iota(jnp.int32, (SUBLANES, A_WIDTH), 0)
    for g in range(G):
        rows = slice(g * TB, (g + 1) * TB)
        ax = z_ref[rows, OFF_AX:OFF_AX + A_WIDTH]
        axbuf[g, SUBLANES:SUBLANES + TB, :] = ax
        u = conv_b_ref[...] + axbuf[g, SUBLANES - 3:SUBLANES - 3 + TB, :] * cw[0:1, :]
        u = u + axbuf[g, SUBLANES - 2:SUBLANES - 2 + TB, :] * cw[1:2, :]
        u = u + axbuf[g, SUBLANES - 1:SUBLANES - 1 + TB, :] * cw[2:3, :]
        u = u + ax * cw[3:4, :]
        if NT > 1:
            axbuf[g, SUBLANES - 3:SUBLANES, :] = axbuf[g, SUBLANES + TB - 3:SUBLANES + TB, :]
        gz = _dot(u.astype(BF16), w_gate_ref[...]) + b_gate_ref[...]
        r = jax.nn.sigmoid(gz[:, 0:A_WIDTH])
        gi = jax.nn.sigmoid(gz[:, A_WIDTH:2 * A_WIDTH])
        log_a = (-LRU_C) * r * softplus_nl
        a = jnp.exp(log_a)
        th = jnp.tanh(log_a)
        one_minus_a2 = (-2.0) * th / (1.0 - th)
        bt = jnp.sqrt(one_minus_a2) * (gi * u)
        sa_ref[rows, :] = a
        sb_ref[rows, :] = bt

        def tile_step(i, hprev, g=g):
            r0 = pl.multiple_of(g * TB + i * SUBLANES, SUBLANES)
            ta = sa_ref[pl.ds(r0, SUBLANES), :]
            tb = sb_ref[pl.ds(r0, SUBLANES), :]
            for s in (1, 2, 4):
                keep = row8 >= s
                tb = jnp.where(keep, ta * pltpu.roll(tb, s, 0) + tb, tb)
                ta = jnp.where(keep, ta * pltpu.roll(ta, s, 0), ta)
            ht = ta * hprev + tb
            sb_ref[pl.ds(r0, SUBLANES), :] = ht
            return jnp.broadcast_to(ht[SUBLANES - 1:SUBLANES, :], (SUBLANES, A_WIDTH))

        h0 = jnp.broadcast_to(hc_ref[g], (SUBLANES, A_WIDTH))
        hlast = lax.fori_loop(0, TB // SUBLANES, tile_step, h0, unroll=4)
        hc_ref[g] = hlast[0:1, :]
        ha = sb_ref[rows, :]
        ag = z_ref[rows, OFF_AG:OFF_AG + A_WIDTH]
        o_ref[rows, 0:A_WIDTH] = ha * jax.nn.gelu(ag)

    lane_head = lax.broadcasted_iota(jnp.int32, (CR, B_WIDTH), 1) // HEAD_DIM
    for g in range(G):
        for c in range(TB // CR):
            rows = slice(g * TB + c * CR, g * TB + (c + 1) * CR)
            q = z_ref[rows, OFF_BQ:OFF_BQ + B_WIDTH]
            k = z_ref[rows, OFF_BK:OFF_BK + B_WIDTH]
            v = z_ref[rows, OFF_BV:OFF_BV + B_WIDTH]
            bgate = z_ref[rows, OFF_BG:OFF_BG + B_WIDTH]
            kx = jnp.concatenate([jnp.where(lane_head == hh, k, 0.0) for hh in range(B_HEADS)], axis=0).astype(BF16)
            vx = jnp.concatenate([jnp.where(lane_head == hh, v, 0.0) for hh in range(B_HEADS)], axis=0).astype(BF16)
            inner = _dot_nt(q.astype(BF16), kx) * dmat_ref[...]
            s_prev = sbd[g]
            ro = _dot(inner.astype(BF16), vx) + _dot((q * wstart_ref[...]).astype(BF16), s_prev.astype(BF16))
            upd = _dot_tn((k * wend_ref[...]).astype(BF16), v.astype(BF16))
            sbd[g] = gbd_ref[...] * s_prev + upd * blk_ref[...]
            mu = _dot(ro.astype(BF16), avg_ref[...])
            dlt = ro - mu
            var = _dot((dlt * dlt).astype(BF16), avg_ref[...])
            on = dlt * lax.rsqrt(var + LN_EPS) * gn_g_ref[...] + gn_b_ref[...]
            o_ref[rows, A_WIDTH:A_WIDTH + B_WIDTH] = on * (bgate * jax.nn.sigmoid(bgate))

    W = WINDOW + QS
    key_id = lax.broadcasted_iota(jnp.int32, (W, C_GROUP * QS), 0)
    for g in range(G):
        rows_g = slice(g * TB, (g + 1) * TB)
        kwin[g, WINDOW:WINDOW + TB, :] = z_ref[rows_g, OFF_CK:OFF_CK + C_KV_WIDTH].astype(BF16)
        vwin[g, WINDOW:WINDOW + TB, :] = z_ref[rows_g, OFF_CV:OFF_CV + C_KV_WIDTH].astype(BF16)
        for c in range(TB // QS):
            rows = slice(g * TB + c * QS, g * TB + (c + 1) * QS)
            kw = kwin[g, c * QS:c * QS + W, :]
            vw = vwin[g, c * QS:c * QS + W, :]
            for kvh in range(C_KV_HEADS):
                qs = jnp.concatenate(
                    [z_ref[rows, OFF_CQ + (kvh * C_GROUP + j) * HEAD_DIM:OFF_CQ + (kvh * C_GROUP + j + 1) * HEAD_DIM]
                     for j in range(C_GROUP)], axis=0)
                qs = (qs * (HEAD_DIM ** -0.5)).astype(BF16)
                s = _dot_nt(kw[:, kvh * HEAD_DIM:(kvh + 1) * HEAD_DIM], qs) - alibi_ref[kvh]
                if (not has_past) and c == 0:
                    n_invalid = jnp.where(t_idx == 0, WINDOW, 0)
                    s = jnp.where(key_id >= n_invalid, s, NEG)
                sk = sink_ref[kvh]
                m = jnp.maximum(jnp.max(s, axis=0, keepdims=True), sk)
                e = jnp.exp(s - m)
                den = jnp.sum(e, axis=0, keepdims=True) + jnp.exp(sk - m)
                ot = _dot_tn(vw, e.astype(BF16))
                ot = ot[kvh * HEAD_DIM:(kvh + 1) * HEAD_DIM, :] * (1.0 / den)
                oc = ot.T
                oc = jnp.concatenate([oc[j * QS:(j + 1) * QS, :] for j in range(C_GROUP)], axis=1)
                lo = A_WIDTH + B_WIDTH + kvh * C_GROUP * HEAD_DIM
                o_ref[rows, lo:lo + C_GROUP * HEAD_DIM] = oc
        if NT > 1:
            kwin[g, 0:WINDOW, :] = kwin[g, TB:TB + WINDOW, :]
            vwin[g, 0:WINDOW, :] = vwin[g, TB:TB + WINDOW, :]

    y = _dot(o_ref[...].astype(BF16), w_out_ref[...])
    h1 = _ln(ALPHA * h + y, ln1_g[...], ln1_b[...])
    h1_ref[...] = h1.reshape(G, TB, D_MODEL)

    @pl.when(t_idx == NT - 1)
    def _fin():
        kn = ok_ref.shape[1]
        for g in range(G):
            oconv_ref[g] = axbuf[g, SUBLANES + TB - 3:SUBLANES + TB, :]
            olru_ref[g] = hc_ref[g]
            for h in range(B_HEADS):
                sl = slice(h * HEAD_DIM, (h + 1) * HEAD_DIM)
                oret_ref[g, h] = sbd[g, sl, sl]
            last = slice((g + 1) * TB - kn, (g + 1) * TB)
            ok_ref[g] = z_ref[last, OFF_CK:OFF_CK + C_KV_WIDTH]
            ov_ref[g] = z_ref[last, OFF_CV:OFF_CV + C_KV_WIDTH]


def _const_spec(shape):
    nd = len(shape)
    return pl.BlockSpec(shape, lambda s, t, _nd=nd: (0,) * _nd, pipeline_mode=pl.Buffered(1))


def _mixer_call(x, states, lp, tabs, *, G, TB, CR, QS, ln_in):
    B, T, _ = x.shape
    NS, NT = B // G, T // TB
    has_past = states is not None
    KN = TB if has_past else WINDOW
    cfg = (G, TB, CR, QS, NT, has_past, ln_in)

    in_specs = [pl.BlockSpec((G, TB, D_MODEL), lambda s, t: (s, t, 0))]
    args = [x]
    if has_past:
        for st in states:
            nd = st.ndim
            in_specs.append(pl.BlockSpec((G,) + st.shape[1:], lambda s, t, _nd=nd: (s,) + (0,) * (_nd - 1)))
            args.append(st)
    consts = list(lp) + list(tabs)
    for cst in consts:
        in_specs.append(_const_spec(cst.shape))
        args.append(cst)

    out_shape = (
        jax.ShapeDtypeStruct((B, T, D_MODEL), F32),
        jax.ShapeDtypeStruct((B, CONV_W - 1, A_WIDTH), F32),
        jax.ShapeDtypeStruct((B, 1, A_WIDTH), F32),
        jax.ShapeDtypeStruct((B, B_HEADS, HEAD_DIM, HEAD_DIM), F32),
        jax.ShapeDtypeStruct((B, KN, C_KV_WIDTH), F32),
        jax.ShapeDtypeStruct((B, KN, C_KV_WIDTH), F32),
    )
    out_specs = (
        pl.BlockSpec((G, TB, D_MODEL), lambda s, t: (s, t, 0)),
        pl.BlockSpec((G, CONV_W - 1, A_WIDTH), lambda s, t: (s, 0, 0)),
        pl.BlockSpec((G, 1, A_WIDTH), lambda s, t: (s, 0, 0)),
        pl.BlockSpec((G, B_HEADS, HEAD_DIM, HEAD_DIM), lambda s, t: (s, 0, 0, 0)),
        pl.BlockSpec((G, KN, C_KV_WIDTH), lambda s, t: (s, 0, 0)),
        pl.BlockSpec((G, KN, C_KV_WIDTH), lambda s, t: (s, 0, 0)),
    )
    R = G * TB
    scratch = [
        pltpu.VMEM((R, D_IN), F32),
        pltpu.VMEM((R, D_MIX), F32),
        pltpu.VMEM((R, A_WIDTH), F32),
        pltpu.VMEM((R, A_WIDTH), F32),
        pltpu.VMEM((G, SUBLANES + TB, A_WIDTH), F32),
        pltpu.VMEM((G, 1, A_WIDTH), F32),
        pltpu.VMEM((G, WINDOW + TB, C_KV_WIDTH), BF16),
        pltpu.VMEM((G, WINDOW + TB, C_KV_WIDTH), BF16),
        pltpu.VMEM((G, B_WIDTH, B_WIDTH), F32),
    ]
    return pl.pallas_call(
        functools.partial(_mixer_body, cfg),
        grid=(NS, NT),
        in_specs=in_specs,
        out_specs=out_specs,
        out_shape=out_shape,
        scratch_shapes=scratch,
        compiler_params=pltpu.CompilerParams(
            dimension_semantics=("arbitrary", "arbitrary"),
            vmem_limit_bytes=VMEM_LIMIT_BYTES),
        name="mixer_past" if has_past else "mixer_prompt",
    )(*args)


FFN_SPLIT = 2
FFN_CHUNK = D_FF // FFN_SPLIT


def _ffn_body(h_ref, wg_ref, wu_ref, wd_ref, g_ref, b_ref, out_ref):
    h = h_ref[...]
    hb = h.astype(BF16)
    acc = jnp.zeros(h.shape, F32)
    for f in range(FFN_SPLIT):
        cols = slice(f * FFN_CHUNK, (f + 1) * FFN_CHUNK)
        a = _dot(hb, wg_ref[:, cols])
        b = _dot(hb, wu_ref[:, cols])
        act = (a * jax.nn.sigmoid(a) * b).astype(BF16)
        acc = acc + _dot(act, wd_ref[cols, :])
    out_ref[...] = _ln(ALPHA * h + acc, g_ref[...], b_ref[...])


def _ffn_call(h2d, wg, wu, wd, g, b, *, TM):
    N = h2d.shape[0]
    cs = lambda shape: pl.BlockSpec(shape, lambda i: (0, 0), pipeline_mode=pl.Buffered(1))
    return pl.pallas_call(
        _ffn_body,
        grid=(N // TM,),
        in_specs=[pl.BlockSpec((TM, D_MODEL), lambda i: (i, 0)),
                  cs((D_MODEL, D_FF)), cs((D_MODEL, D_FF)), cs((D_FF, D_MODEL)),
                  cs((1, D_MODEL)), cs((1, D_MODEL))],
        out_specs=pl.BlockSpec((TM, D_MODEL), lambda i: (i, 0)),
        out_shape=jax.ShapeDtypeStruct((N, D_MODEL), F32),
        compiler_params=pltpu.CompilerParams(
            dimension_semantics=("arbitrary",),
            vmem_limit_bytes=VMEM_LIMIT_BYTES),
        name="ffn",
    )(h2d, wg, wu, wd, g, b)


def _block_diag(w):
    hh = w.shape[0]
    eye = jnp.eye(hh, dtype=w.dtype)
    return (eye[:, None, :, None] * w[:, :, None, :]).reshape(hh * HEAD_DIM, hh * HEAD_DIM)


def kernel(x_prompt, x_sample, state_conv, state_lru, state_ret, cache_k, cache_v, ln_in_g, ln_in_b, w_in, conv_w, conv_b, w_rg_a, b_rg_a, w_rg_x, b_rg_x, lru_lambda, ret_gn_g, ret_gn_b, sinks, w_out, ln1_g, ln1_b, w_gate, w_up, w_down, ln2_g, ln2_b):
    row = lambda v: v.reshape(1, -1)
    w_in_b = w_in.astype(BF16)
    w_out_b = w_out.astype(BF16)
    w_gate_b = w_gate.astype(BF16)
    w_up_b = w_up.astype(BF16)
    w_down_b = w_down.astype(BF16)

    PROMPT = dict(G=1, TB=512, CR=256, QS=128)
    SAMPLE = dict(G=8, TB=32, CR=32, QS=32)
    tabs_p = _retention_tables(PROMPT["CR"]) + (_alibi_table(PROMPT["QS"]),)
    tabs_s = _retention_tables(SAMPLE["CR"]) + (_alibi_table(SAMPLE["QS"]),)

    def layer_params(l, chunk):
        w_gates = jnp.concatenate([_block_diag(w_rg_a[l]), _block_diag(w_rg_x[l])], axis=1).astype(BF16)
        b_gates = jnp.concatenate([b_rg_a[l], b_rg_x[l]]).reshape(1, -1)
        sink_rows = jnp.repeat(sinks[l], chunk).reshape(C_KV_HEADS, 1, C_GROUP * chunk)
        return (row(ln_in_g), row(ln_in_b), w_in_b[l], conv_w[l], row(conv_b[l]), w_gates, b_gates,
                row(lru_lambda[l]), row(ret_gn_g[l]), row(ret_gn_b[l]), sink_rows, w_out_b[l],
                row(ln1_g[l]), row(ln1_b[l]))

    hp, hs = x_prompt, x_sample
    outs_p, outs_s = [], []
    nb_p, nb_s = x_prompt.shape[0], x_sample.shape[0]
    wc = cache_k.shape[2]
    for l in range(DEPTH):
        res = _mixer_call(hp, None, layer_params(l, PROMPT["QS"]), tabs_p, ln_in=(l == 0), **PROMPT)
        hp, st = res[0], res[1:]
        outs_p.append(st)
        hp = _ffn_call(hp.reshape(-1, D_MODEL), w_gate_b[l], w_up_b[l], w_down_b[l], row(ln2_g[l]), row(ln2_b[l]),
                       TM=512).reshape(x_prompt.shape)
        states = (state_conv[l], state_lru[l].reshape(nb_s, 1, A_WIDTH), state_ret[l],
                  cache_k[l].reshape(nb_s, wc, C_KV_WIDTH), cache_v[l].reshape(nb_s, wc, C_KV_WIDTH))
        res = _mixer_call(hs, states, layer_params(l, SAMPLE["QS"]), tabs_s, ln_in=(l == 0), **SAMPLE)
        hs, st = res[0], res[1:]
        outs_s.append(st)
        hs = _ffn_call(hs.reshape(-1, D_MODEL), w_gate_b[l], w_up_b[l], w_down_b[l], row(ln2_g[l]), row(ln2_b[l]),
                       TM=512).reshape(x_sample.shape)

    def stack(outs, nb, kn):
        conv = jnp.stack([o[0] for o in outs])
        lru = jnp.stack([o[1].reshape(nb, A_WIDTH) for o in outs])
        ret = jnp.stack([o[2] for o in outs])
        kk = jnp.stack([o[3].reshape(nb, kn, C_KV_HEADS, HEAD_DIM) for o in outs])
        vv = jnp.stack([o[4].reshape(nb, kn, C_KV_HEADS, HEAD_DIM) for o in outs])
        return conv, lru, ret, kk, vv

    p_conv, p_lru, p_ret, p_k, p_v = stack(outs_p, nb_p, WINDOW)
    s_conv, s_lru, s_ret, s_k, s_v = stack(outs_s, nb_s, SAMPLE["TB"])
    return (hp, hs, p_conv, p_lru, p_ret, p_k, p_v, s_conv, s_lru, s_ret, s_k, s_v)
```

```python
import functools
import math

import numpy as np
import jax
import jax.numpy as jnp
from jax import lax
from jax.experimental import pallas as pl
from jax.experimental.pallas import tpu as pltpu

F32 = jnp.float32
BF16 = jnp.bfloat16

D_MODEL = 1024
DEPTH = 4
HEAD_DIM = 64
A_WIDTH = 256
B_HEADS = 4
B_WIDTH = 256
C_HEADS = 8
C_KV_HEADS = 2
C_GROUP = 4
C_WIDTH = 512
C_KV_WIDTH = 128
CONV_W = 4
LRU_C = 8.0
WINDOW = 128
ATTN_CHUNK = 64
D_MIX = 1024
D_IN = 2304
D_FF = 2816
ALPHA = (2.0 * DEPTH) ** 0.25
LN_EPS = 1e-5
NEG = -1e30

OFF_AX, OFF_AG = 0, 256
OFF_BQ, OFF_BK, OFF_BV, OFF_BG = 512, 768, 1024, 1280
OFF_CQ, OFF_CK, OFF_CV = 1536, 2048, 2176

SUBLANES = 8
VMEM_LIMIT_BYTES = 52 * 1024 * 1024


def _ln(x, g, b):
    mu = jnp.mean(x, axis=-1, keepdims=True)
    d = x - mu
    var = jnp.mean(d * d, axis=-1, keepdims=True)
    return d * lax.rsqrt(var + LN_EPS) * g + b


def _dot(a, b):
    return jnp.dot(a, b, preferred_element_type=F32)


def _dot_nt(a, b):
    return lax.dot_general(a, b, (((1,), (1,)), ((), ())), preferred_element_type=F32)


def _dot_tn(a, b):
    return lax.dot_general(a, b, (((0,), (0,)), ((), ())), preferred_element_type=F32)


def _retention_tables(chunk):
    lg = np.log1p(-(2.0 ** (-5.0 - np.arange(B_HEADS, dtype=np.float64))))
    idx = np.arange(chunk, dtype=np.float64)
    diff = idx[:, None] - idx[None, :]
    dmat = np.where(diff >= 0, np.exp(np.maximum(diff, 0.0)[None] * lg[:, None, None]), 0.0)
    dmat = dmat * HEAD_DIM ** -0.5
    dmat_all = np.concatenate([dmat[h] for h in range(B_HEADS)], axis=1)
    w_start = np.exp((idx + 1.0)[:, None] * lg[None, :])
    w_end = np.exp((chunk - 1.0 - idx)[:, None] * lg[None, :]) * HEAD_DIM ** -0.5
    w_start = np.repeat(w_start, HEAD_DIM, axis=1)
    w_end = np.repeat(w_end, HEAD_DIM, axis=1)
    g_chunk = np.repeat(np.exp(chunk * lg), HEAD_DIM)
    head = np.arange(B_WIDTH) // HEAD_DIM
    blk = (head[:, None] == head[None, :]).astype(np.float64)
    g_bd = blk * g_chunk[:, None]
    avg_bd = blk / HEAD_DIM
    return (jnp.asarray(dmat_all, F32), jnp.asarray(w_start, F32), jnp.asarray(w_end, F32),
            jnp.asarray(g_bd, F32), jnp.asarray(blk, F32), jnp.asarray(avg_bd, BF16))


def _alibi_table(qs):
    w = WINDOW + qs
    slopes = 2.0 ** (-8.0 * np.arange(1, C_HEADS + 1, dtype=np.float64) / C_HEADS)
    qpos = np.arange(qs)
    kpos = np.arange(w) - WINDOW
    dist = np.abs(qpos[:, None] - kpos[None, :]).astype(np.float64)
    q_chunk = qpos // ATTN_CHUNK
    k_chunk = np.floor_divide(kpos, ATTN_CHUNK)
    visible = (k_chunk[None, :] <= q_chunk[:, None]) & (k_chunk[None, :] >= q_chunk[:, None] - WINDOW // ATTN_CHUNK)
    tab = slopes[:, None, None] * dist[None]
    tab = np.where(visible[None], tab, -NEG)
    tab = tab.reshape(C_KV_HEADS, C_GROUP * qs, w).transpose(0, 2, 1)
    return jnp.asarray(tab, F32)


def _mixer_body(cfg, *refs):
    G, TB, CR, QS, NT, has_past, ln_in = cfg
    R = G * TB
    it = iter(refs)
    x_ref = next(it)
    if has_past:
        sconv_ref, slru_ref, sret_ref, ck_ref, cv_ref = (next(it) for _ in range(5))
    (ln_in_g, ln_in_b, w_in_ref, conv_w_ref, conv_b_ref, w_gate_ref, b_gate_ref, lam_ref,
     gn_g_ref, gn_b_ref, sink_ref, w_out_ref, ln1_g, ln1_b,
     dmat_ref, wstart_ref, wend_ref, gbd_ref, blk_ref, avg_ref, alibi_ref) = (next(it) for _ in range(21))
    h1_ref, oconv_ref, olru_ref, oret_ref, ok_ref, ov_ref = (next(it) for _ in range(6))
    z_ref, o_ref, sa_ref, sb_ref, axbuf, hc_ref, kwin, vwin, sbd = (next(it) for _ in range(9))

    t_idx = pl.program_id(1)

    @pl.when(t_idx == 0)
    def _init():
        for g in range(G):
            if has_past:
                axbuf[g, SUBLANES - 3:SUBLANES, :] = sconv_ref[g]
                hc_ref[g] = slru_ref[g]
                sbd[g] = jnp.zeros((B_WIDTH, B_WIDTH), F32)
                for h in range(B_HEADS):
                    sl = slice(h * HEAD_DIM, (h + 1) * HEAD_DIM)
                    sbd[g, sl, sl] = sret_ref[g, h]
                kwin[g, 0:WINDOW, :] = ck_ref[g].astype(BF16)
                vwin[g, 0:WINDOW, :] = cv_ref[g].astype(BF16)
            else:
                axbuf[g, 0:SUBLANES, :] = jnp.zeros((SUBLANES, A_WIDTH), F32)
                hc_ref[g] = jnp.zeros((1, A_WIDTH), F32)
                sbd[g] = jnp.zeros((B_WIDTH, B_WIDTH), F32)
                kwin[g, 0:WINDOW, :] = jnp.zeros((WINDOW, C_KV_WIDTH), BF16)
                vwin[g, 0:WINDOW, :] = jnp.zeros((WINDOW, C_KV_WIDTH), BF16)

    x = x_ref[...].reshape(R, D_MODEL)
    if ln_in:
        h = _ln(x, ln_in_g[...], ln_in_b[...])
    else:
        h = x
    z_ref[...] = _dot(h.astype(BF16), w_in_ref[...])

    lam = lam_ref[...]
    nl = -lam
    softplus_nl = jnp.maximum(nl, 0.0) + jnp.log1p(jnp.exp(-jnp.abs(nl)))
    cw = conv_w_ref[...]
    row_in_tile = lax.broadcasted_iota(jnp.int32, (TB // SUBLANES, SUBLANES, A_WIDTH), 1)
    SCAN_UNROLL = min(8, TB // SUBLANES)
    for g in range(G):
        rows = slice(g * TB, (g + 1) * TB)
        ax = z_ref[rows, OFF_AX:OFF_AX + A_WIDTH]
        axbuf[g, SUBLANES:SUBLANES + TB, :] = ax
        u = conv_b_ref[...] + axbuf[g, SUBLANES - 3:SUBLANES - 3 + TB, :] * cw[0:1, :]
        u = u + axbuf[g, SUBLANES - 2:SUBLANES - 2 + TB, :] * cw[1:2, :]
        u = u + axbuf[g, SUBLANES - 1:SUBLANES - 1 + TB, :] * cw[2:3, :]
        u = u + ax * cw[3:4, :]
        if NT > 1:
            axbuf[g, SUBLANES - 3:SUBLANES, :] = axbuf[g, SUBLANES + TB - 3:SUBLANES + TB, :]
        gz = _dot(u.astype(BF16), w_gate_ref[...]) + b_gate_ref[...]
        r = jax.nn.sigmoid(gz[:, 0:A_WIDTH])
        gi = jax.nn.sigmoid(gz[:, A_WIDTH:2 * A_WIDTH])
        log_a = (-LRU_C) * r * softplus_nl
        a = jnp.exp(log_a)
        th = jnp.tanh(log_a)
        one_minus_a2 = (-2.0) * th / (1.0 - th)
        bt = jnp.sqrt(one_minus_a2) * (gi * u)
        a = a.reshape(TB // SUBLANES, SUBLANES, A_WIDTH)
        bt = bt.reshape(TB // SUBLANES, SUBLANES, A_WIDTH)
        for s in (1, 2, 4):
            keep = row_in_tile >= s
            bt = jnp.where(keep, a * pltpu.roll(bt, s, 1) + bt, bt)
            a = jnp.where(keep, a * pltpu.roll(a, s, 1), a)
        sa_ref[rows, :] = a.reshape(TB, A_WIDTH)
        sb_ref[rows, :] = bt.reshape(TB, A_WIDTH)

        def tile_step(i, hprev, g=g):
            r0 = pl.multiple_of(g * TB + i * SUBLANES, SUBLANES)
            ht = sa_ref[pl.ds(r0, SUBLANES), :] * hprev + sb_ref[pl.ds(r0, SUBLANES), :]
            sb_ref[pl.ds(r0, SUBLANES), :] = ht
            return jnp.broadcast_to(ht[SUBLANES - 1:SUBLANES, :], (SUBLANES, A_WIDTH))

        h0 = jnp.broadcast_to(hc_ref[g], (SUBLANES, A_WIDTH))
        hlast = lax.fori_loop(0, TB // SUBLANES, tile_step, h0, unroll=SCAN_UNROLL)
        hc_ref[g] = hlast[0:1, :]
        ha = sb_ref[rows, :]
        ag = z_ref[rows, OFF_AG:OFF_AG + A_WIDTH]
        o_ref[rows, 0:A_WIDTH] = ha * jax.nn.gelu(ag)

    lane_head = lax.broadcasted_iota(jnp.int32, (CR, B_WIDTH), 1) // HEAD_DIM
    for g in range(G):
        for c in range(TB // CR):
            rows = slice(g * TB + c * CR, g * TB + (c + 1) * CR)
            q = z_ref[rows, OFF_BQ:OFF_BQ + B_WIDTH]
            k = z_ref[rows, OFF_BK:OFF_BK + B_WIDTH]
            v = z_ref[rows, OFF_BV:OFF_BV + B_WIDTH]
            bgate = z_ref[rows, OFF_BG:OFF_BG + B_WIDTH]
            kx = jnp.concatenate([jnp.where(lane_head == hh, k, 0.0) for hh in range(B_HEADS)], axis=0).astype(BF16)
            vx = jnp.concatenate([jnp.where(lane_head == hh, v, 0.0) for hh in range(B_HEADS)], axis=0).astype(BF16)
            inner = _dot_nt(q.astype(BF16), kx) * dmat_ref[...]
            s_prev = sbd[g]
            ro = _dot(inner.astype(BF16), vx) + _dot((q * wstart_ref[...]).astype(BF16), s_prev.astype(BF16))
            upd = _dot_tn((k * wend_ref[...]).astype(BF16), v.astype(BF16))
            sbd[g] = gbd_ref[...] * s_prev + upd * blk_ref[...]
            mu = _dot(ro.astype(BF16), avg_ref[...])
            dlt = ro - mu
            var = _dot((dlt * dlt).astype(BF16), avg_ref[...])
            on = dlt * lax.rsqrt(var + LN_EPS) * gn_g_ref[...] + gn_b_ref[...]
            o_ref[rows, A_WIDTH:A_WIDTH + B_WIDTH] = on * (bgate * jax.nn.sigmoid(bgate))

    W = WINDOW + QS
    key_id = lax.broadcasted_iota(jnp.int32, (W, C_GROUP * QS), 0)
    for g in range(G):
        rows_g = slice(g * TB, (g + 1) * TB)
        kwin[g, WINDOW:WINDOW + TB, :] = z_ref[rows_g, OFF_CK:OFF_CK + C_KV_WIDTH].astype(BF16)
        vwin[g, WINDOW:WINDOW + TB, :] = z_ref[rows_g, OFF_CV:OFF_CV + C_KV_WIDTH].astype(BF16)
        for c in range(TB // QS):
            rows = slice(g * TB + c * QS, g * TB + (c + 1) * QS)
            kw = kwin[g, c * QS:c * QS + W, :]
            vw = vwin[g, c * QS:c * QS + W, :]
            for kvh in range(C_KV_HEADS):
                qs = jnp.concatenate(
                    [z_ref[rows, OFF_CQ + (kvh * C_GROUP + j) * HEAD_DIM:OFF_CQ + (kvh * C_GROUP + j + 1) * HEAD_DIM]
                     for j in range(C_GROUP)], axis=0)
                qs = (qs * (HEAD_DIM ** -0.5)).astype(BF16)
                s = _dot_nt(kw[:, kvh * HEAD_DIM:(kvh + 1) * HEAD_DIM], qs) - alibi_ref[kvh]
                if (not has_past) and c == 0:
                    n_invalid = jnp.where(t_idx == 0, WINDOW, 0)
                    s = jnp.where(key_id >= n_invalid, s, NEG)
                sk = sink_ref[kvh]
                m = jnp.maximum(jnp.max(s, axis=0, keepdims=True), sk)
                e = jnp.exp(s - m)
                den = jnp.sum(e, axis=0, keepdims=True) + jnp.exp(sk - m)
                ot = _dot_tn(vw, e.astype(BF16))
                ot = ot[kvh * HEAD_DIM:(kvh + 1) * HEAD_DIM, :] * (1.0 / den)
                oc = ot.T
                oc = jnp.concatenate([oc[j * QS:(j + 1) * QS, :] for j in range(C_GROUP)], axis=1)
                lo = A_WIDTH + B_WIDTH + kvh * C_GROUP * HEAD_DIM
                o_ref[rows, lo:lo + C_GROUP * HEAD_DIM] = oc
        if NT > 1:
            kwin[g, 0:WINDOW, :] = kwin[g, TB:TB + WINDOW, :]
            vwin[g, 0:WINDOW, :] = vwin[g, TB:TB + WINDOW, :]

    y = _dot(o_ref[...].astype(BF16), w_out_ref[...])
    h1 = _ln(ALPHA * h + y, ln1_g[...], ln1_b[...])
    h1_ref[...] = h1.reshape(G, TB, D_MODEL)

    @pl.when(t_idx == NT - 1)
    def _fin():
        kn = ok_ref.shape[1]
        for g in range(G):
            oconv_ref[g] = axbuf[g, SUBLANES + TB - 3:SUBLANES + TB, :]
            olru_ref[g] = hc_ref[g]
            for h in range(B_HEADS):
                sl = slice(h * HEAD_DIM, (h + 1) * HEAD_DIM)
                oret_ref[g, h] = sbd[g, sl, sl]
            last = slice((g + 1) * TB - kn, (g + 1) * TB)
            ok_ref[g] = z_ref[last, OFF_CK:OFF_CK + C_KV_WIDTH]
            ov_ref[g] = z_ref[last, OFF_CV:OFF_CV + C_KV_WIDTH]


def _const_spec(shape):
    nd = len(shape)
    return pl.BlockSpec(shape, lambda *_, _nd=nd: (0,) * _nd, pipeline_mode=pl.Buffered(1))


def _layer_spec(shape, layer):
    nd = len(shape) - 1
    return pl.BlockSpec((None,) + tuple(shape[1:]), lambda *_, _nd=nd: (layer,) + (0,) * _nd,
                        pipeline_mode=pl.Buffered(1))


def _mixer_call(x, states, layer, lp, tabs, *, G, TB, CR, QS, ln_in):
    B, T, _ = x.shape
    NS, NT = B // G, T // TB
    has_past = states is not None
    KN = TB if has_past else WINDOW
    cfg = (G, TB, CR, QS, NT, has_past, ln_in)

    in_specs = [pl.BlockSpec((G, TB, D_MODEL), lambda s, t: (s, t, 0))]
    args = [x]
    if has_past:
        for st in states:
            nd = st.ndim
            in_specs.append(pl.BlockSpec((None, G) + st.shape[2:],
                                         lambda s, t, _nd=nd: (layer, s) + (0,) * (_nd - 2)))
            args.append(st)
    for cst, stacked in lp:
        in_specs.append(_layer_spec(cst.shape, layer) if stacked else _const_spec(cst.shape))
        args.append(cst)
    for cst in tabs:
        in_specs.append(_const_spec(cst.shape))
        args.append(cst)

    out_shape = (
        jax.ShapeDtypeStruct((B, T, D_MODEL), F32),
        jax.ShapeDtypeStruct((B, CONV_W - 1, A_WIDTH), F32),
        jax.ShapeDtypeStruct((B, 1, A_WIDTH), F32),
        jax.ShapeDtypeStruct((B, B_HEADS, HEAD_DIM, HEAD_DIM), F32),
        jax.ShapeDtypeStruct((B, KN, C_KV_WIDTH), F32),
        jax.ShapeDtypeStruct((B, KN, C_KV_WIDTH), F32),
    )
    out_specs = (
        pl.BlockSpec((G, TB, D_MODEL), lambda s, t: (s, t, 0)),
        pl.BlockSpec((G, CONV_W - 1, A_WIDTH), lambda s, t: (s, 0, 0)),
        pl.BlockSpec((G, 1, A_WIDTH), lambda s, t: (s, 0, 0)),
        pl.BlockSpec((G, B_HEADS, HEAD_DIM, HEAD_DIM), lambda s, t: (s, 0, 0, 0)),
        pl.BlockSpec((G, KN, C_KV_WIDTH), lambda s, t: (s, 0, 0)),
        pl.BlockSpec((G, KN, C_KV_WIDTH), lambda s, t: (s, 0, 0)),
    )
    R = G * TB
    scratch = [
        pltpu.VMEM((R, D_IN), F32),
        pltpu.VMEM((R, D_MIX), F32),
        pltpu.VMEM((R, A_WIDTH), F32),
        pltpu.VMEM((R, A_WIDTH), F32),
        pltpu.VMEM((G, SUBLANES + TB, A_WIDTH), F32),
        pltpu.VMEM((G, 1, A_WIDTH), F32),
        pltpu.VMEM((G, WINDOW + TB, C_KV_WIDTH), BF16),
        pltpu.VMEM((G, WINDOW + TB, C_KV_WIDTH), BF16),
        pltpu.VMEM((G, B_WIDTH, B_WIDTH), F32),
    ]
    return pl.pallas_call(
        functools.partial(_mixer_body, cfg),
        grid=(NS, NT),
        in_specs=in_specs,
        out_specs=out_specs,
        out_shape=out_shape,
        scratch_shapes=scratch,
        compiler_params=pltpu.CompilerParams(
            dimension_semantics=("arbitrary", "arbitrary"),
            vmem_limit_bytes=VMEM_LIMIT_BYTES),
        name="mixer_past" if has_past else "mixer_prompt",
    )(*args)


FFN_SPLIT = 2
FFN_CHUNK = D_FF // FFN_SPLIT


def _ffn_body(h_ref, wg_ref, wu_ref, wd_ref, g_ref, b_ref, out_ref):
    h = h_ref[...]
    hb = h.astype(BF16)
    acc = jnp.zeros(h.shape, F32)
    for f in range(FFN_SPLIT):
        cols = slice(f * FFN_CHUNK, (f + 1) * FFN_CHUNK)
        a = _dot(hb, wg_ref[:, cols])
        b = _dot(hb, wu_ref[:, cols])
        act = (a * jax.nn.sigmoid(a) * b).astype(BF16)
        acc = acc + _dot(act, wd_ref[cols, :])
    out_ref[...] = _ln(ALPHA * h + acc, g_ref[...], b_ref[...])


def _ffn_call(h2d, layer, wg, wu, wd, g, b, *, TM):
    N = h2d.shape[0]
    return pl.pallas_call(
        _ffn_body,
        grid=(N // TM,),
        in_specs=[pl.BlockSpec((TM, D_MODEL), lambda i: (i, 0))]
                 + [_layer_spec(w.shape, layer) for w in (wg, wu, wd, g, b)],
        out_specs=pl.BlockSpec((TM, D_MODEL), lambda i: (i, 0)),
        out_shape=jax.ShapeDtypeStruct((N, D_MODEL), F32),
        compiler_params=pltpu.CompilerParams(
            dimension_semantics=("arbitrary",),
            vmem_limit_bytes=VMEM_LIMIT_BYTES),
        name="ffn",
    )(h2d, wg, wu, wd, g, b)


def _block_diag(w):
    nl, hh = w.shape[0], w.shape[1]
    eye = jnp.eye(hh, dtype=w.dtype)
    return (eye[None, :, None, :, None] * w[:, :, :, None, :]).reshape(nl, hh * HEAD_DIM, hh * HEAD_DIM)


def kernel(x_prompt, x_sample, state_conv, state_lru, state_ret, cache_k, cache_v, ln_in_g, ln_in_b, w_in, conv_w, conv_b, w_rg_a, b_rg_a, w_rg_x, b_rg_x, lru_lambda, ret_gn_g, ret_gn_b, sinks, w_out, ln1_g, ln1_b, w_gate, w_up, w_down, ln2_g, ln2_b):
    nb_p, nb_s = x_prompt.shape[0], x_sample.shape[0]
    wc = cache_k.shape[2]
    rows = lambda v: v.reshape(DEPTH, 1, -1)
    w_in_b = w_in.astype(BF16)
    w_out_b = w_out.astype(BF16)
    w_gate_b = w_gate.astype(BF16)
    w_up_b = w_up.astype(BF16)
    w_down_b = w_down.astype(BF16)
    ln2_gr, ln2_br = rows(ln2_g), rows(ln2_b)

    PROMPT = dict(G=1, TB=512, CR=256, QS=128)
    SAMPLE = dict(G=8, TB=32, CR=32, QS=32)
    tabs_p = _retention_tables(PROMPT["CR"]) + (_alibi_table(PROMPT["QS"]),)
    tabs_s = _retention_tables(SAMPLE["CR"]) + (_alibi_table(SAMPLE["QS"]),)

    w_gates = jnp.concatenate([_block_diag(w_rg_a), _block_diag(w_rg_x)], axis=2).astype(BF16)
    b_gates = jnp.concatenate([b_rg_a, b_rg_x], axis=1).reshape(DEPTH, 1, 2 * A_WIDTH)
    shared = [(w_in_b, True), (conv_w, True), (rows(conv_b), True), (w_gates, True), (b_gates, True),
              (rows(lru_lambda), True), (rows(ret_gn_g), True), (rows(ret_gn_b), True)]
    tail = [(w_out_b, True), (rows(ln1_g), True), (rows(ln1_b), True)]
    head = [(ln_in_g.reshape(1, -1), False), (ln_in_b.reshape(1, -1), False)]

    def mixer_params(qs):
        sink_rows = jnp.repeat(sinks, qs, axis=1).reshape(DEPTH, C_KV_HEADS, 1, C_GROUP * qs)
        return head + shared + [(sink_rows, True)] + tail

    lp_p, lp_s = mixer_params(PROMPT["QS"]), mixer_params(SAMPLE["QS"])
    states = (state_conv, state_lru.reshape(DEPTH, nb_s, 1, A_WIDTH), state_ret,
              cache_k.reshape(DEPTH, nb_s, wc, C_KV_WIDTH), cache_v.reshape(DEPTH, nb_s, wc, C_KV_WIDTH))

    hp, hs = x_prompt, x_sample
    outs_p, outs_s = [], []
    for l in range(DEPTH):
        res = _mixer_call(hp, None, l, lp_p, tabs_p, ln_in=(l == 0), **PROMPT)
        hp, st = res[0], res[1:]
        outs_p.append(st)
        hp = _ffn_call(hp.reshape(-1, D_MODEL), l, w_gate_b, w_up_b, w_down_b, ln2_gr, ln2_br,
                       TM=512).reshape(x_prompt.shape)
        res = _mixer_call(hs, states, l, lp_s, tabs_s, ln_in=(l == 0), **SAMPLE)
        hs, st = res[0], res[1:]
        outs_s.append(st)
        hs = _ffn_call(hs.reshape(-1, D_MODEL), l, w_gate_b, w_up_b, w_down_b, ln2_gr, ln2_br,
                       TM=512).reshape(x_sample.shape)

    def stack(outs, nb, kn):
        conv = jnp.stack([o[0] for o in outs])
        lru = jnp.stack([o[1].reshape(nb, A_WIDTH) for o in outs])
        ret = jnp.stack([o[2] for o in outs])
        kk = jnp.stack([o[3].reshape(nb, kn, C_KV_HEADS, HEAD_DIM) for o in outs])
        vv = jnp.stack([o[4].reshape(nb, kn, C_KV_HEADS, HEAD_DIM) for o in outs])
        return conv, lru, ret, kk, vv

    p_conv, p_lru, p_ret, p_k, p_v = stack(outs_p, nb_p, WINDOW)
    s_conv, s_lru, s_ret, s_k, s_v = stack(outs_s, nb_s, SAMPLE["TB"])
    return (hp, hs, p_conv, p_lru, p_ret, p_k, p_v, s_conv, s_lru, s_ret, s_k, s_v)
```

```python
import functools
import math

import numpy as np
import jax
import jax.numpy as jnp
from jax import lax
from jax.experimental import pallas as pl
from jax.experimental.pallas import tpu as pltpu

F32 = jnp.float32
BF16 = jnp.bfloat16

D_MODEL = 1024
DEPTH = 4
HEAD_DIM = 64
A_WIDTH = 256
B_HEADS = 4
B_WIDTH = 256
C_HEADS = 8
C_KV_HEADS = 2
C_GROUP = 4
C_WIDTH = 512
C_KV_WIDTH = 128
CONV_W = 4
LRU_C = 8.0
WINDOW = 128
ATTN_CHUNK = 64
D_MIX = 1024
D_IN = 2304
D_FF = 2816
ALPHA = (2.0 * DEPTH) ** 0.25
LN_EPS = 1e-5
NEG = -1e30
LOG2E = math.log2(math.e)

OFF_AX, OFF_AG = 0, 256
OFF_BQ, OFF_BK, OFF_BV, OFF_BG = 512, 768, 1024, 1280
OFF_CQ, OFF_CK, OFF_CV = 1536, 2048, 2176
PROJ_SPLIT = 768

SUBLANES = 8
VMEM_LIMIT_BYTES = 52 * 1024 * 1024


def _ln(x, g, b):
    mu = jnp.mean(x, axis=-1, keepdims=True)
    d = x - mu
    var = jnp.mean(d * d, axis=-1, keepdims=True)
    return d * lax.rsqrt(var + LN_EPS) * g + b


def _dot(a, b):
    return jnp.dot(a, b, preferred_element_type=F32)


def _dot_nt(a, b):
    return lax.dot_general(a, b, (((1,), (1,)), ((), ())), preferred_element_type=F32)


def _dot_tn(a, b):
    return lax.dot_general(a, b, (((0,), (0,)), ((), ())), preferred_element_type=F32)


def _retention_tables(chunk):
    lg = np.log1p(-(2.0 ** (-5.0 - np.arange(B_HEADS, dtype=np.float64))))
    idx = np.arange(chunk, dtype=np.float64)
    diff = idx[:, None] - idx[None, :]
    dmat = np.where(diff >= 0, np.exp(np.maximum(diff, 0.0)[None] * lg[:, None, None]), 0.0)
    dmat = dmat * HEAD_DIM ** -0.5
    dmat_all = np.concatenate([dmat[h] for h in range(B_HEADS)], axis=1)
    w_start = np.exp((idx + 1.0)[:, None] * lg[None, :])
    w_end = np.exp((chunk - 1.0 - idx)[:, None] * lg[None, :]) * HEAD_DIM ** -0.5
    w_start = np.repeat(w_start, HEAD_DIM, axis=1)
    w_end = np.repeat(w_end, HEAD_DIM, axis=1)
    g_chunk = np.repeat(np.exp(chunk * lg), HEAD_DIM)
    head = np.arange(B_WIDTH) // HEAD_DIM
    blk = (head[:, None] == head[None, :]).astype(np.float64)
    g_bd = blk * g_chunk[:, None]
    avg_bd = blk / HEAD_DIM
    return (jnp.asarray(dmat_all, F32), jnp.asarray(w_start, F32), jnp.asarray(w_end, F32),
            jnp.asarray(g_bd, F32), jnp.asarray(blk, F32), jnp.asarray(avg_bd, BF16))


def _alibi_table(qs):
    w = WINDOW + qs
    slopes = 2.0 ** (-8.0 * np.arange(1, C_HEADS + 1, dtype=np.float64) / C_HEADS)
    qpos = np.arange(qs)
    kpos = np.arange(w) - WINDOW
    dist = np.abs(qpos[:, None] - kpos[None, :]).astype(np.float64)
    q_chunk = qpos // ATTN_CHUNK
    k_chunk = np.floor_divide(kpos, ATTN_CHUNK)
    visible = (k_chunk[None, :] <= q_chunk[:, None]) & (k_chunk[None, :] >= q_chunk[:, None] - WINDOW // ATTN_CHUNK)
    tab = slopes[:, None, None] * dist[None]
    tab = np.where(visible[None], tab * LOG2E, -NEG)
    tab = tab.reshape(C_KV_HEADS, C_GROUP * qs, w).transpose(0, 2, 1)
    return jnp.asarray(tab, F32)


def _mixer_body(cfg, *refs):
    G, TB, CR, QS, NT, NSUB, has_past, ln_in = cfg
    R = G * TB
    W = WINDOW + QS
    carried = NT * NSUB > 1
    it = iter(refs)
    x_ref = next(it)
    if NSUB > 1:
        xnext_ref = next(it)
    if has_past:
        sconv_ref, slru_ref, sret_ref, ck_ref, cv_ref = (next(it) for _ in range(5))
    (ln_in_g, ln_in_b, w_in_ref, conv_w_ref, conv_b_ref, w_gate_ref, b_gate_ref, lam_ref,
     gn_g_ref, gn_b_ref, sink_ref, w_out_ref, ln1_g, ln1_b,
     dmat_ref, wstart_ref, wend_ref, gbd_ref, blk_ref, avg_ref, alibi_ref) = (next(it) for _ in range(21))
    h1_ref, oconv_ref, olru_ref, oret_ref, ok_ref, ov_ref = (next(it) for _ in range(6))
    z_refs = [next(it) for _ in range(NSUB)]
    o_all, sb_all, axbuf, hc_ref, kwin, vwin, sbd = (next(it) for _ in range(7))
    if NSUB > 1:
        hb_ref = next(it)

    t_idx = pl.program_id(1)

    def normed(xv):
        return _ln(xv, ln_in_g[...], ln_in_b[...]) if ln_in else xv

    def x_rows(sub):
        if NSUB == 1:
            return x_ref[...].reshape(R, D_MODEL)
        return x_ref[0, sub * TB:(sub + 1) * TB, :]

    @pl.when(t_idx == 0)
    def _init():
        for g in range(G):
            if has_past:
                axbuf[g, SUBLANES - 3:SUBLANES, :] = sconv_ref[g]
                hc_ref[g] = slru_ref[g]
                sbd[g] = jnp.zeros((B_WIDTH, B_WIDTH), F32)
                for h in range(B_HEADS):
                    sl = slice(h * HEAD_DIM, (h + 1) * HEAD_DIM)
                    sbd[g, sl, sl] = sret_ref[g, h]
                kwin[g, 0:WINDOW, :] = ck_ref[g].astype(BF16)
                vwin[g, 0:WINDOW, :] = cv_ref[g].astype(BF16)
            else:
                axbuf[g, 0:SUBLANES, :] = jnp.zeros((SUBLANES, A_WIDTH), F32)
                hc_ref[g] = jnp.zeros((1, A_WIDTH), F32)
                sbd[g] = jnp.zeros((B_WIDTH, B_WIDTH), F32)
                kwin[g, 0:WINDOW, :] = jnp.zeros((WINDOW, C_KV_WIDTH), BF16)
                vwin[g, 0:WINDOW, :] = jnp.zeros((WINDOW, C_KV_WIDTH), BF16)
        if NSUB > 1:
            z_refs[0][...] = _dot(normed(x_rows(0)).astype(BF16), w_in_ref[...])

    def group_a(z_ref, o_ref, sb_ref):
        nl = -lam_ref[...]
        softplus_nl = jnp.maximum(nl, 0.0) + jnp.log1p(jnp.exp(-jnp.abs(nl)))
        cw = conv_w_ref[...]
        row_in_tile = lax.broadcasted_iota(jnp.int32, (TB // SUBLANES, SUBLANES, A_WIDTH), 1)
        for g in range(G):
            rows = slice(g * TB, (g + 1) * TB)
            ax = z_ref[rows, OFF_AX:OFF_AX + A_WIDTH]
            axbuf[g, SUBLANES:SUBLANES + TB, :] = ax
            u = conv_b_ref[...] + axbuf[g, SUBLANES - 3:SUBLANES - 3 + TB, :] * cw[0:1, :]
            u = u + axbuf[g, SUBLANES - 2:SUBLANES - 2 + TB, :] * cw[1:2, :]
            u = u + axbuf[g, SUBLANES - 1:SUBLANES - 1 + TB, :] * cw[2:3, :]
            u = u + ax * cw[3:4, :]
            if carried:
                axbuf[g, SUBLANES - 3:SUBLANES, :] = axbuf[g, SUBLANES + TB - 3:SUBLANES + TB, :]
            gz = _dot(u.astype(BF16), w_gate_ref[...]) + b_gate_ref[...]
            r = jax.nn.sigmoid(gz[:, 0:A_WIDTH])
            gi = jax.nn.sigmoid(gz[:, A_WIDTH:2 * A_WIDTH])
            log_a = (-LRU_C) * r * softplus_nl
            a = jnp.exp(log_a)
            th = jnp.tanh(log_a)
            one_minus_a2 = (-2.0) * th / (1.0 - th)
            bt = jnp.sqrt(one_minus_a2) * (gi * u)
            a = a.reshape(TB // SUBLANES, SUBLANES, A_WIDTH)
            bt = bt.reshape(TB // SUBLANES, SUBLANES, A_WIDTH)
            for s in (1, 2, 4):
                keep = row_in_tile >= s
                bt = jnp.where(keep, a * pltpu.roll(bt, s, 1) + bt, bt)
                a = jnp.where(keep, a * pltpu.roll(a, s, 1), a)
            hprev = jnp.broadcast_to(hc_ref[g], (SUBLANES, A_WIDTH))
            for i in range(TB // SUBLANES):
                ht = a[i] * hprev + bt[i]
                sb_ref[g * TB + i * SUBLANES:g * TB + (i + 1) * SUBLANES, :] = ht
                hprev = jnp.broadcast_to(ht[SUBLANES - 1:SUBLANES, :], (SUBLANES, A_WIDTH))
            hc_ref[g] = hprev[0:1, :]
            ha = sb_ref[rows, :]
            ag = z_ref[rows, OFF_AG:OFF_AG + A_WIDTH]
            o_ref[rows, 0:A_WIDTH] = ha * jax.nn.gelu(ag)

    def group_b(z_ref, o_ref):
        lane_head = lax.broadcasted_iota(jnp.int32, (CR, B_WIDTH), 1) // HEAD_DIM
        for g in range(G):
            for c in range(TB // CR):
                rows = slice(g * TB + c * CR, g * TB + (c + 1) * CR)
                q = z_ref[rows, OFF_BQ:OFF_BQ + B_WIDTH]
                k = z_ref[rows, OFF_BK:OFF_BK + B_WIDTH]
                v = z_ref[rows, OFF_BV:OFF_BV + B_WIDTH]
                bgate = z_ref[rows, OFF_BG:OFF_BG + B_WIDTH]
                kx = jnp.concatenate([jnp.where(lane_head == hh, k, 0.0) for hh in range(B_HEADS)], axis=0).astype(BF16)
                vx = jnp.concatenate([jnp.where(lane_head == hh, v, 0.0) for hh in range(B_HEADS)], axis=0).astype(BF16)
                inner = _dot_nt(q.astype(BF16), kx) * dmat_ref[...]
                s_prev = sbd[g]
                ro = _dot(inner.astype(BF16), vx) + _dot((q * wstart_ref[...]).astype(BF16), s_prev.astype(BF16))
                upd = _dot_tn((k * wend_ref[...]).astype(BF16), v.astype(BF16))
                sbd[g] = gbd_ref[...] * s_prev + upd * blk_ref[...]
                mu = _dot(ro.astype(BF16), avg_ref[...])
                dlt = ro - mu
                var = _dot((dlt * dlt).astype(BF16), avg_ref[...])
                on = dlt * lax.rsqrt(var + LN_EPS) * gn_g_ref[...] + gn_b_ref[...]
                o_ref[rows, A_WIDTH:A_WIDTH + B_WIDTH] = on * (bgate * jax.nn.sigmoid(bgate))

    def group_c(z_ref, o_ref, stream_start):
        for g in range(G):
            rows_g = slice(g * TB, (g + 1) * TB)
            kwin[g, WINDOW:WINDOW + TB, :] = z_ref[rows_g, OFF_CK:OFF_CK + C_KV_WIDTH].astype(BF16)
            vwin[g, WINDOW:WINDOW + TB, :] = z_ref[rows_g, OFF_CV:OFF_CV + C_KV_WIDTH].astype(BF16)
            for c in range(TB // QS):
                rows = slice(g * TB + c * QS, g * TB + (c + 1) * QS)
                kw = kwin[g, c * QS:c * QS + W, :]
                vw = vwin[g, c * QS:c * QS + W, :]
                for kvh in range(C_KV_HEADS):
                    qs = jnp.concatenate(
                        [z_ref[rows, OFF_CQ + (kvh * C_GROUP + j) * HEAD_DIM:OFF_CQ + (kvh * C_GROUP + j + 1) * HEAD_DIM]
                         for j in range(C_GROUP)], axis=0)
                    qs = (qs * (HEAD_DIM ** -0.5 * LOG2E)).astype(BF16)
                    s = _dot_nt(kw[:, kvh * HEAD_DIM:(kvh + 1) * HEAD_DIM], qs) - alibi_ref[kvh]
                    if stream_start is not None and c == 0:
                        key_id = lax.broadcasted_iota(jnp.int32, (W, C_GROUP * QS), 0)
                        n_invalid = jnp.where(stream_start, WINDOW, 0)
                        s = jnp.where(key_id >= n_invalid, s, NEG)
                    sk = sink_ref[kvh]
                    m = jnp.maximum(jnp.max(s, axis=0, keepdims=True), sk)
                    e = jnp.exp2(s - m)
                    den = jnp.sum(e, axis=0, keepdims=True) + jnp.exp2(sk - m)
                    ot = _dot_tn(vw, e.astype(BF16))
                    ot = ot[kvh * HEAD_DIM:(kvh + 1) * HEAD_DIM, :] * (1.0 / den)
                    oc = ot.T
                    oc = jnp.concatenate([oc[j * QS:(j + 1) * QS, :] for j in range(C_GROUP)], axis=1)
                    lo = A_WIDTH + B_WIDTH + kvh * C_GROUP * HEAD_DIM
                    o_ref[rows, lo:lo + C_GROUP * HEAD_DIM] = oc
            if carried:
                kwin[g, 0:WINDOW, :] = kwin[g, TB:TB + WINDOW, :]
                vwin[g, 0:WINDOW, :] = vwin[g, TB:TB + WINDOW, :]

    def out_proj(sub, o_ref):
        y = _dot(o_ref[...].astype(BF16), w_out_ref[...])
        h1 = _ln(ALPHA * normed(x_rows(sub)) + y, ln1_g[...], ln1_b[...])
        if NSUB == 1:
            h1_ref[...] = h1.reshape(G, TB, D_MODEL)
        else:
            h1_ref[0, sub * TB:(sub + 1) * TB, :] = h1

    def new_states(z_ref):
        kn = ok_ref.shape[1]
        for g in range(G):
            oconv_ref[g] = axbuf[g, SUBLANES + TB - 3:SUBLANES + TB, :]
            olru_ref[g] = hc_ref[g]
            for h in range(B_HEADS):
                sl = slice(h * HEAD_DIM, (h + 1) * HEAD_DIM)
                oret_ref[g, h] = sbd[g, sl, sl]
            last = slice((g + 1) * TB - kn, (g + 1) * TB)
            ok_ref[g] = z_ref[last, OFF_CK:OFF_CK + C_KV_WIDTH]
            ov_ref[g] = z_ref[last, OFF_CV:OFF_CV + C_KV_WIDTH]

    if NSUB == 1:
        z_ref, o_ref, sb_ref = z_refs[0], o_all.at[0], sb_all.at[0]
        z_ref[...] = _dot(normed(x_rows(0)).astype(BF16), w_in_ref[...])
        group_a(z_ref, o_ref, sb_ref)
        group_b(z_ref, o_ref)
        group_c(z_ref, o_ref, None if has_past else (t_idx == 0))
        out_proj(0, o_ref)
        pl.when(t_idx == NT - 1)(functools.partial(new_states, z_ref))
        return

    n_segments = [0]

    def always():
        n_segments[0] += 1
        return t_idx > -n_segments[0]

    for sub in range(NSUB):
        z_ref, o_ref, sb_ref = z_refs[sub], o_all.at[sub], sb_all.at[sub]
        z_next = z_refs[(sub + 1) % NSUB]

        @pl.when(always())
        def _seg_a(sub=sub, z_ref=z_ref, o_ref=o_ref, sb_ref=sb_ref, z_next=z_next):
            x_next = x_rows(sub + 1) if sub + 1 < NSUB else xnext_ref[0]
            hb = normed(x_next).astype(BF16)
            hb_ref[...] = hb
            z_next[:, 0:PROJ_SPLIT] = _dot(hb, w_in_ref[:, 0:PROJ_SPLIT])
            group_a(z_ref, o_ref, sb_ref)

        @pl.when(always())
        def _seg_b(z_ref=z_ref, o_ref=o_ref):
            group_b(z_ref, o_ref)

        @pl.when(always())
        def _seg_c(sub=sub, z_ref=z_ref, o_ref=o_ref, z_next=z_next):
            z_next[:, PROJ_SPLIT:D_IN] = _dot(hb_ref[...], w_in_ref[:, PROJ_SPLIT:D_IN])
            group_c(z_ref, o_ref, (t_idx == 0) if sub == 0 else None)

        @pl.when(always())
        def _seg_out(sub=sub, o_ref=o_ref):
            out_proj(sub, o_ref)

    pl.when(t_idx == NT - 1)(functools.partial(new_states, z_refs[NSUB - 1]))


def _const_spec(shape):
    nd = len(shape)
    return pl.BlockSpec(shape, lambda *_, _nd=nd: (0,) * _nd, pipeline_mode=pl.Buffered(1))


def _layer_spec(shape, layer):
    nd = len(shape) - 1
    return pl.BlockSpec((None,) + tuple(shape[1:]), lambda *_, _nd=nd: (layer,) + (0,) * _nd,
                        pipeline_mode=pl.Buffered(1))


def _mixer_call(x, states, layer, lp, tabs, *, G, TB, CR, QS, NSUB, ln_in):
    B, T, _ = x.shape
    NS, NT = B // G, T // (NSUB * TB)
    has_past = states is not None
    assert NSUB == 1 or G == 1
    KN = TB if has_past else WINDOW
    cfg = (G, TB, CR, QS, NT, NSUB, has_past, ln_in)

    in_specs = [pl.BlockSpec((G, NSUB * TB, D_MODEL), lambda s, t: (s, t, 0))]
    args = [x]
    if NSUB > 1:
        last_block = T // TB - 1
        in_specs.append(pl.BlockSpec((G, TB, D_MODEL), lambda s, t: (s, jnp.minimum(NSUB * (t + 1), last_block), 0)))
        args.append(x)
    if has_past:
        for st in states:
            nd = st.ndim
            in_specs.append(pl.BlockSpec((None, G) + st.shape[2:],
                                         lambda s, t, _nd=nd: (layer, s) + (0,) * (_nd - 2)))
            args.append(st)
    for cst, stacked in lp:
        in_specs.append(_layer_spec(cst.shape, layer) if stacked else _const_spec(cst.shape))
        args.append(cst)
    for cst in tabs:
        in_specs.append(_const_spec(cst.shape))
        args.append(cst)

    out_shape = (
        jax.ShapeDtypeStruct((B, T, D_MODEL), F32),
        jax.ShapeDtypeStruct((B, CONV_W - 1, A_WIDTH), F32),
        jax.ShapeDtypeStruct((B, 1, A_WIDTH), F32),
        jax.ShapeDtypeStruct((B, B_HEADS, HEAD_DIM, HEAD_DIM), F32),
        jax.ShapeDtypeStruct((B, KN, C_KV_WIDTH), F32),
        jax.ShapeDtypeStruct((B, KN, C_KV_WIDTH), F32),
    )
    out_specs = (
        pl.BlockSpec((G, NSUB * TB, D_MODEL), lambda s, t: (s, t, 0)),
        pl.BlockSpec((G, CONV_W - 1, A_WIDTH), lambda s, t: (s, 0, 0)),
        pl.BlockSpec((G, 1, A_WIDTH), lambda s, t: (s, 0, 0)),
        pl.BlockSpec((G, B_HEADS, HEAD_DIM, HEAD_DIM), lambda s, t: (s, 0, 0, 0)),
        pl.BlockSpec((G, KN, C_KV_WIDTH), lambda s, t: (s, 0, 0)),
        pl.BlockSpec((G, KN, C_KV_WIDTH), lambda s, t: (s, 0, 0)),
    )
    R = G * TB
    scratch = [pltpu.VMEM((R, D_IN), F32) for _ in range(NSUB)]
    scratch += [
        pltpu.VMEM((NSUB, R, D_MIX), F32),
        pltpu.VMEM((NSUB, R, A_WIDTH), F32),
        pltpu.VMEM((G, SUBLANES + TB, A_WIDTH), F32),
        pltpu.VMEM((G, 1, A_WIDTH), F32),
        pltpu.VMEM((G, WINDOW + TB, C_KV_WIDTH), BF16),
        pltpu.VMEM((G, WINDOW + TB, C_KV_WIDTH), BF16),
        pltpu.VMEM((G, B_WIDTH, B_WIDTH), F32),
    ]
    if NSUB > 1:
        scratch.append(pltpu.VMEM((R, D_MODEL), BF16))
    return pl.pallas_call(
        functools.partial(_mixer_body, cfg),
        grid=(NS, NT),
        in_specs=in_specs,
        out_specs=out_specs,
        out_shape=out_shape,
        scratch_shapes=scratch,
        compiler_params=pltpu.CompilerParams(
            dimension_semantics=("arbitrary", "arbitrary"),
            vmem_limit_bytes=VMEM_LIMIT_BYTES),
        name="mixer_past" if has_past else "mixer_prompt",
    )(*args)


FFN_SPLIT = 2
FFN_CHUNK = D_FF // FFN_SPLIT


def _ffn_body(h_ref, wg_ref, wu_ref, wd_ref, g_ref, b_ref, out_ref):
    h = h_ref[...]
    hb = h.astype(BF16)
    acc = jnp.zeros(h.shape, F32)
    for f in range(FFN_SPLIT):
        cols = slice(f * FFN_CHUNK, (f + 1) * FFN_CHUNK)
        a = _dot(hb, wg_ref[:, cols])
        b = _dot(hb, wu_ref[:, cols])
        act = (a * jax.nn.sigmoid(a) * b).astype(BF16)
        acc = acc + _dot(act, wd_ref[cols, :])
    out_ref[...] = _ln(ALPHA * h + acc, g_ref[...], b_ref[...])


def _ffn_call(h2d, layer, wg, wu, wd, g, b, *, TM):
    N = h2d.shape[0]
    return pl.pallas_call(
        _ffn_body,
        grid=(N // TM,),
        in_specs=[pl.BlockSpec((TM, D_MODEL), lambda i: (i, 0))]
                 + [_layer_spec(w.shape, layer) for w in (wg, wu, wd, g, b)],
        out_specs=pl.BlockSpec((TM, D_MODEL), lambda i: (i, 0)),
        out_shape=jax.ShapeDtypeStruct((N, D_MODEL), F32),
        compiler_params=pltpu.CompilerParams(
            dimension_semantics=("arbitrary",),
            vmem_limit_bytes=VMEM_LIMIT_BYTES),
        name="ffn",
    )(h2d, wg, wu, wd, g, b)


def _block_diag(w):
    nl, hh = w.shape[0], w.shape[1]
    eye = jnp.eye(hh, dtype=w.dtype)
    return (eye[None, :, None, :, None] * w[:, :, :, None, :]).reshape(nl, hh * HEAD_DIM, hh * HEAD_DIM)


def kernel(x_prompt, x_sample, state_conv, state_lru, state_ret, cache_k, cache_v, ln_in_g, ln_in_b, w_in, conv_w, conv_b, w_rg_a, b_rg_a, w_rg_x, b_rg_x, lru_lambda, ret_gn_g, ret_gn_b, sinks, w_out, ln1_g, ln1_b, w_gate, w_up, w_down, ln2_g, ln2_b):
    nb_p, nb_s = x_prompt.shape[0], x_sample.shape[0]
    wc = cache_k.shape[2]
    rows = lambda v: v.reshape(DEPTH, 1, -1)
    w_in_b = w_in.astype(BF16)
    w_out_b = w_out.astype(BF16)
    w_gate_b = w_gate.astype(BF16)
    w_up_b = w_up.astype(BF16)
    w_down_b = w_down.astype(BF16)
    ln2_gr, ln2_br = rows(ln2_g), rows(ln2_b)

    PROMPT = dict(G=1, TB=1024, CR=256, QS=128, NSUB=1)
    SAMPLE = dict(G=8, TB=32, CR=32, QS=32, NSUB=1)
    tabs_p = _retention_tables(PROMPT["CR"]) + (_alibi_table(PROMPT["QS"]),)
    tabs_s = _retention_tables(SAMPLE["CR"]) + (_alibi_table(SAMPLE["QS"]),)

    w_gates = jnp.concatenate([_block_diag(w_rg_a), _block_diag(w_rg_x)], axis=2).astype(BF16)
    b_gates = jnp.concatenate([b_rg_a, b_rg_x], axis=1).reshape(DEPTH, 1, 2 * A_WIDTH)
    shared = [(w_in_b, True), (conv_w, True), (rows(conv_b), True), (w_gates, True), (b_gates, True),
              (rows(lru_lambda), True), (rows(ret_gn_g), True), (rows(ret_gn_b), True)]
    tail = [(w_out_b, True), (rows(ln1_g), True), (rows(ln1_b), True)]
    head = [(ln_in_g.reshape(1, -1), False), (ln_in_b.reshape(1, -1), False)]

    def mixer_params(qs):
        sink_rows = jnp.repeat(sinks * LOG2E, qs, axis=1).reshape(DEPTH, C_KV_HEADS, 1, C_GROUP * qs)
        return head + shared + [(sink_rows, True)] + tail

    lp_p, lp_s = mixer_params(PROMPT["QS"]), mixer_params(SAMPLE["QS"])
    states = (state_conv, state_lru.reshape(DEPTH, nb_s, 1, A_WIDTH), state_ret,
              cache_k.reshape(DEPTH, nb_s, wc, C_KV_WIDTH), cache_v.reshape(DEPTH, nb_s, wc, C_KV_WIDTH))

    hp, hs = x_prompt, x_sample
    outs_p, outs_s = [], []
    for l in range(DEPTH):
        res = _mixer_call(hp, None, l, lp_p, tabs_p, ln_in=(l == 0), **PROMPT)
        hp, st = res[0], res[1:]
        outs_p.append(st)
        hp = _ffn_call(hp.reshape(-1, D_MODEL), l, w_gate_b, w_up_b, w_down_b, ln2_gr, ln2_br,
                       TM=512).reshape(x_prompt.shape)
        res = _mixer_call(hs, states, l, lp_s, tabs_s, ln_in=(l == 0), **SAMPLE)
        hs, st = res[0], res[1:]
        outs_s.append(st)
        hs = _ffn_call(hs.reshape(-1, D_MODEL), l, w_gate_b, w_up_b, w_down_b, ln2_gr, ln2_br,
                       TM=512).reshape(x_sample.shape)

    def stack(outs, nb, kn):
        conv = jnp.stack([o[0] for o in outs])
        lru = jnp.stack([o[1].reshape(nb, A_WIDTH) for o in outs])
        ret = jnp.stack([o[2] for o in outs])
        kk = jnp.stack([o[3].reshape(nb, kn, C_KV_HEADS, HEAD_DIM) for o in outs])
        vv = jnp.stack([o[4].reshape(nb, kn, C_KV_HEADS, HEAD_DIM) for o in outs])
        return conv, lru, ret, kk, vv

    p_conv, p_lru, p_ret, p_k, p_v = stack(outs_p, nb_p, WINDOW)
    s_conv, s_lru, s_ret, s_k, s_v = stack(outs_s, nb_s, SAMPLE["TB"])
    return (hp, hs, p_conv, p_lru, p_ret, p_k, p_v, s_conv, s_lru, s_ret, s_k, s_v)
```

```python
import functools
import math

import numpy as np
import jax
import jax.numpy as jnp
from jax import lax
from jax.experimental import pallas as pl
from jax.experimental.pallas import tpu as pltpu

F32 = jnp.float32
BF16 = jnp.bfloat16

D_MODEL = 1024
DEPTH = 4
HEAD_DIM = 64
A_WIDTH = 256
B_HEADS = 4
B_WIDTH = 256
C_HEADS = 8
C_KV_HEADS = 2
C_GROUP = 4
C_WIDTH = 512
C_KV_WIDTH = 128
CONV_W = 4
LRU_C = 8.0
WINDOW = 128
ATTN_CHUNK = 64
D_MIX = 1024
D_IN = 2304
D_FF = 2816
ALPHA = (2.0 * DEPTH) ** 0.25
LN_EPS = 1e-5
NEG = -1e30
LOG2E = math.log2(math.e)

OFF_AX, OFF_AG = 0, 256
OFF_BQ, OFF_BK, OFF_BV, OFF_BG = 512, 768, 1024, 1280
OFF_CQ, OFF_CK, OFF_CV = 1536, 2048, 2176
PROJ_SPLIT = 768

SUBLANES = 8
VMEM_LIMIT_BYTES = 52 * 1024 * 1024


def _ln(x, g, b):
    mu = jnp.mean(x, axis=-1, keepdims=True)
    d = x - mu
    var = jnp.mean(d * d, axis=-1, keepdims=True)
    return d * lax.rsqrt(var + LN_EPS) * g + b


def _dot(a, b):
    return jnp.dot(a, b, preferred_element_type=F32)


def _dot_nt(a, b):
    return lax.dot_general(a, b, (((1,), (1,)), ((), ())), preferred_element_type=F32)


def _dot_tn(a, b):
    return lax.dot_general(a, b, (((0,), (0,)), ((), ())), preferred_element_type=F32)


def _retention_tables(chunk):
    lg = np.log1p(-(2.0 ** (-5.0 - np.arange(B_HEADS, dtype=np.float64))))
    idx = np.arange(chunk, dtype=np.float64)
    diff = idx[:, None] - idx[None, :]
    dmat = np.where(diff >= 0, np.exp(np.maximum(diff, 0.0)[None] * lg[:, None, None]), 0.0)
    dmat = dmat * HEAD_DIM ** -0.5
    dmat_all = np.concatenate([dmat[h] for h in range(B_HEADS)], axis=1)
    w_start = np.exp((idx + 1.0)[:, None] * lg[None, :])
    w_end = np.exp((chunk - 1.0 - idx)[:, None] * lg[None, :]) * HEAD_DIM ** -0.5
    w_start = np.repeat(w_start, HEAD_DIM, axis=1)
    w_end = np.repeat(w_end, HEAD_DIM, axis=1)
    g_chunk = np.repeat(np.exp(chunk * lg), HEAD_DIM)
    head = np.arange(B_WIDTH) // HEAD_DIM
    blk = (head[:, None] == head[None, :]).astype(np.float64)
    g_bd = blk * g_chunk[:, None]
    avg_bd = blk / HEAD_DIM
    return (jnp.asarray(dmat_all, F32), jnp.asarray(w_start, F32), jnp.asarray(w_end, F32),
            jnp.asarray(g_bd, F32), jnp.asarray(blk, F32), jnp.asarray(avg_bd, BF16))


def _alibi_table(qs):
    w = WINDOW + qs
    slopes = 2.0 ** (-8.0 * np.arange(1, C_HEADS + 1, dtype=np.float64) / C_HEADS)
    qpos = np.arange(qs)
    kpos = np.arange(w) - WINDOW
    dist = np.abs(qpos[:, None] - kpos[None, :]).astype(np.float64)
    q_chunk = qpos // ATTN_CHUNK
    k_chunk = np.floor_divide(kpos, ATTN_CHUNK)
    visible = (k_chunk[None, :] <= q_chunk[:, None]) & (k_chunk[None, :] >= q_chunk[:, None] - WINDOW // ATTN_CHUNK)
    tab = slopes[:, None, None] * dist[None]
    tab = np.where(visible[None], tab * LOG2E, -NEG)
    tab = tab.reshape(C_KV_HEADS, C_GROUP * qs, w).transpose(0, 2, 1)
    return jnp.asarray(tab, F32)


def _mixer_body(cfg, *refs):
    G, TB, CR, QS, NT, NSUB, has_past, ln_in = cfg
    R = G * TB
    W = WINDOW + QS
    carried = NT * NSUB > 1
    it = iter(refs)
    x_ref = next(it)
    if NSUB > 1:
        xnext_ref = next(it)
    if has_past:
        sconv_ref, slru_ref, sret_ref, ck_ref, cv_ref = (next(it) for _ in range(5))
    (ln_in_g, ln_in_b, w_in_ref, conv_w_ref, conv_b_ref, w_gate_ref, b_gate_ref, lam_ref,
     gn_g_ref, gn_b_ref, sink_ref, w_out_ref, ln1_g, ln1_b,
     dmat_ref, wstart_ref, wend_ref, gbd_ref, blk_ref, avg_ref, alibi_ref) = (next(it) for _ in range(21))
    h1_ref, oconv_ref, olru_ref, oret_ref, ok_ref, ov_ref = (next(it) for _ in range(6))
    z_refs = [next(it) for _ in range(NSUB)]
    o_all, sb_all, axbuf, hc_ref, kwin, vwin, sbd = (next(it) for _ in range(7))
    if NSUB > 1:
        hb_ref = next(it)

    t_idx = pl.program_id(1)

    def normed(xv):
        return _ln(xv, ln_in_g[...], ln_in_b[...]) if ln_in else xv

    def x_rows(sub):
        if NSUB == 1:
            return x_ref[...].reshape(R, D_MODEL)
        return x_ref[0, sub * TB:(sub + 1) * TB, :]

    @pl.when(t_idx == 0)
    def _init():
        for g in range(G):
            if has_past:
                axbuf[g, SUBLANES - 3:SUBLANES, :] = sconv_ref[g]
                hc_ref[g] = slru_ref[g]
                sbd[g] = jnp.zeros((B_WIDTH, B_WIDTH), F32)
                for h in range(B_HEADS):
                    sl = slice(h * HEAD_DIM, (h + 1) * HEAD_DIM)
                    sbd[g, sl, sl] = sret_ref[g, h]
                kwin[g, 0:WINDOW, :] = ck_ref[g].astype(BF16)
                vwin[g, 0:WINDOW, :] = cv_ref[g].astype(BF16)
            else:
                axbuf[g, 0:SUBLANES, :] = jnp.zeros((SUBLANES, A_WIDTH), F32)
                hc_ref[g] = jnp.zeros((1, A_WIDTH), F32)
                sbd[g] = jnp.zeros((B_WIDTH, B_WIDTH), F32)
                kwin[g, 0:WINDOW, :] = jnp.zeros((WINDOW, C_KV_WIDTH), BF16)
                vwin[g, 0:WINDOW, :] = jnp.zeros((WINDOW, C_KV_WIDTH), BF16)
        if NSUB > 1:
            z_refs[0][...] = _dot(normed(x_rows(0)).astype(BF16), w_in_ref[...])

    def group_a(z_ref, o_ref, sb_ref):
        nl = -lam_ref[...]
        softplus_nl = jnp.maximum(nl, 0.0) + jnp.log1p(jnp.exp(-jnp.abs(nl)))
        cw = conv_w_ref[...]
        row_in_tile = lax.broadcasted_iota(jnp.int32, (TB // SUBLANES, SUBLANES, A_WIDTH), 1)
        for g in range(G):
            rows = slice(g * TB, (g + 1) * TB)
            ax = z_ref[rows, OFF_AX:OFF_AX + A_WIDTH]
            axbuf[g, SUBLANES:SUBLANES + TB, :] = ax
            u = conv_b_ref[...] + axbuf[g, SUBLANES - 3:SUBLANES - 3 + TB, :] * cw[0:1, :]
            u = u + axbuf[g, SUBLANES - 2:SUBLANES - 2 + TB, :] * cw[1:2, :]
            u = u + axbuf[g, SUBLANES - 1:SUBLANES - 1 + TB, :] * cw[2:3, :]
            u = u + ax * cw[3:4, :]
            if carried:
                axbuf[g, SUBLANES - 3:SUBLANES, :] = axbuf[g, SUBLANES + TB - 3:SUBLANES + TB, :]
            gz = _dot(u.astype(BF16), w_gate_ref[...]) + b_gate_ref[...]
            r = jax.nn.sigmoid(gz[:, 0:A_WIDTH])
            gi = jax.nn.sigmoid(gz[:, A_WIDTH:2 * A_WIDTH])
            log_a = (-LRU_C) * r * softplus_nl
            a = jnp.exp(log_a)
            th = jnp.tanh(log_a)
            one_minus_a2 = (-2.0) * th / (1.0 - th)
            bt = jnp.sqrt(one_minus_a2) * (gi * u)
            a = a.reshape(TB // SUBLANES, SUBLANES, A_WIDTH)
            bt = bt.reshape(TB // SUBLANES, SUBLANES, A_WIDTH)
            for s in (1, 2, 4):
                keep = row_in_tile >= s
                bt = jnp.where(keep, a * pltpu.roll(bt, s, 1) + bt, bt)
                a = jnp.where(keep, a * pltpu.roll(a, s, 1), a)
            hprev = jnp.broadcast_to(hc_ref[g], (SUBLANES, A_WIDTH))
            for i in range(TB // SUBLANES):
                ht = a[i] * hprev + bt[i]
                sb_ref[g * TB + i * SUBLANES:g * TB + (i + 1) * SUBLANES, :] = ht
                hprev = jnp.broadcast_to(ht[SUBLANES - 1:SUBLANES, :], (SUBLANES, A_WIDTH))
            hc_ref[g] = hprev[0:1, :]
            ha = sb_ref[rows, :]
            ag = z_ref[rows, OFF_AG:OFF_AG + A_WIDTH]
            o_ref[rows, 0:A_WIDTH] = ha * jax.nn.gelu(ag)

    def group_b(z_ref, o_ref):
        lane_head = lax.broadcasted_iota(jnp.int32, (CR, B_WIDTH), 1) // HEAD_DIM
        for g in range(G):
            for c in range(TB // CR):
                rows = slice(g * TB + c * CR, g * TB + (c + 1) * CR)
                q = z_ref[rows, OFF_BQ:OFF_BQ + B_WIDTH]
                k = z_ref[rows, OFF_BK:OFF_BK + B_WIDTH]
                v = z_ref[rows, OFF_BV:OFF_BV + B_WIDTH]
                kx =jnp.concatenate([jnp.where(lane_head == hh, k, 0.0) for hh in range(B_HEADS)], axis=0).astype(BF16)
                vx = jnp.concatenate([jnp.where(lane_head == hh, v, 0.0) for hh in range(B_HEADS)], axis=0).astype(BF16)
                inner = _dot_nt(q.astype(BF16), kx) * dmat_ref[...]
                s_prev = sbd[g]
                ro = _dot(inner.astype(BF16), vx) + _dot((q * wstart_ref[...]).astype(BF16), s_prev.astype(BF16))
                upd = _dot_tn((k * wend_ref[...]).astype(BF16), v.astype(BF16))
                sbd[g] = gbd_ref[...] * s_prev + upd * blk_ref[...]
                o_ref[rows, A_WIDTH:A_WIDTH + B_WIDTH] = ro
        ro = o_ref[:, A_WIDTH:A_WIDTH + B_WIDTH]
        bgate = z_ref[:, OFF_BG:OFF_BG + B_WIDTH]
        mu = _dot(ro.astype(BF16), avg_ref[...])
        dlt = ro - mu
        var = _dot((dlt * dlt).astype(BF16), avg_ref[...])
        on = dlt * lax.rsqrt(var + LN_EPS) * gn_g_ref[...] + gn_b_ref[...]
        o_ref[:, A_WIDTH:A_WIDTH + B_WIDTH] = on * (bgate * jax.nn.sigmoid(bgate))

    def group_c(z_ref, o_ref, stream_start):
        for g in range(G):
            rows_g = slice(g * TB, (g + 1) * TB)
            kwin[g, WINDOW:WINDOW + TB, :] = z_ref[rows_g, OFF_CK:OFF_CK + C_KV_WIDTH].astype(BF16)
            vwin[g, WINDOW:WINDOW + TB, :] = z_ref[rows_g, OFF_CV:OFF_CV + C_KV_WIDTH].astype(BF16)
            for c in range(TB // QS):
                rows = slice(g * TB + c * QS, g * TB + (c + 1) * QS)
                kw = kwin[g, c * QS:c * QS + W, :]
                vw = vwin[g, c * QS:c * QS + W, :]
                for kvh in range(C_KV_HEADS):
                    qs = jnp.concatenate(
                        [z_ref[rows, OFF_CQ + (kvh * C_GROUP + j) * HEAD_DIM:OFF_CQ + (kvh * C_GROUP + j + 1) * HEAD_DIM]
                         for j in range(C_GROUP)], axis=0)
                    qs = (qs * (HEAD_DIM ** -0.5 * LOG2E)).astype(BF16)
                    s = _dot_nt(kw[:, kvh * HEAD_DIM:(kvh + 1) * HEAD_DIM], qs) - alibi_ref[kvh]
                    if stream_start is not None and c == 0:
                        key_id = lax.broadcasted_iota(jnp.int32, (W, C_GROUP * QS), 0)
                        n_invalid = jnp.where(stream_start, WINDOW, 0)
                        s = jnp.where(key_id >= n_invalid, s, NEG)
                    sk = sink_ref[kvh]
                    m = jnp.maximum(jnp.max(s, axis=0, keepdims=True), sk)
                    e = jnp.exp2(s - m)
                    den = jnp.sum(e, axis=0, keepdims=True) + jnp.exp2(sk - m)
                    ot = _dot_tn(vw, e.astype(BF16))
                    ot = ot[kvh * HEAD_DIM:(kvh + 1) * HEAD_DIM, :] * (1.0 / den)
                    oc = ot.T
                    oc = jnp.concatenate([oc[j * QS:(j + 1) * QS, :] for j in range(C_GROUP)], axis=1)
                    lo = A_WIDTH + B_WIDTH + kvh * C_GROUP * HEAD_DIM
                    o_ref[rows, lo:lo + C_GROUP * HEAD_DIM] = oc
            if carried:
                kwin[g, 0:WINDOW, :] = kwin[g, TB:TB + WINDOW, :]
                vwin[g, 0:WINDOW, :] = vwin[g, TB:TB + WINDOW, :]

    def out_proj(sub, o_ref):
        y = _dot(o_ref[...].astype(BF16), w_out_ref[...])
        h1 = _ln(ALPHA * normed(x_rows(sub)) + y, ln1_g[...], ln1_b[...])
        if NSUB == 1:
            h1_ref[...] = h1.reshape(G, TB, D_MODEL)
        else:
            h1_ref[0, sub * TB:(sub + 1) * TB, :] = h1

    def new_states(z_ref):
        kn = ok_ref.shape[1]
        for g in range(G):
            oconv_ref[g] = axbuf[g, SUBLANES + TB - 3:SUBLANES + TB, :]
            olru_ref[g] = hc_ref[g]
            for h in range(B_HEADS):
                sl = slice(h * HEAD_DIM, (h + 1) * HEAD_DIM)
                oret_ref[g, h] = sbd[g, sl, sl]
            last = slice((g + 1) * TB - kn, (g + 1) * TB)
            ok_ref[g] = z_ref[last, OFF_CK:OFF_CK + C_KV_WIDTH]
            ov_ref[g] = z_ref[last, OFF_CV:OFF_CV + C_KV_WIDTH]

    if NSUB == 1:
        z_ref, o_ref, sb_ref = z_refs[0], o_all.at[0], sb_all.at[0]
        z_ref[...] = _dot(normed(x_rows(0)).astype(BF16), w_in_ref[...])
        group_a(z_ref, o_ref, sb_ref)
        group_b(z_ref, o_ref)
        group_c(z_ref, o_ref, None if has_past else (t_idx == 0))
        out_proj(0, o_ref)
        pl.when(t_idx == NT - 1)(functools.partial(new_states, z_ref))
        return

    n_segments = [0]

    def always():
        n_segments[0] += 1
        return t_idx > -n_segments[0]

    for sub in range(NSUB):
        z_ref, o_ref, sb_ref = z_refs[sub], o_all.at[sub], sb_all.at[sub]
        z_next = z_refs[(sub + 1) % NSUB]

        @pl.when(always())
        def _seg_a(sub=sub, z_ref=z_ref, o_ref=o_ref, sb_ref=sb_ref, z_next=z_next):
            x_next = x_rows(sub + 1) if sub + 1 < NSUB else xnext_ref[0]
            hb = normed(x_next).astype(BF16)
            hb_ref[...] = hb
            z_next[:, 0:PROJ_SPLIT] = _dot(hb, w_in_ref[:, 0:PROJ_SPLIT])
            group_a(z_ref, o_ref, sb_ref)

        @pl.when(always())
        def _seg_b(z_ref=z_ref, o_ref=o_ref):
            group_b(z_ref, o_ref)

        @pl.when(always())
        def _seg_c(sub=sub, z_ref=z_ref, o_ref=o_ref, z_next=z_next):
            z_next[:, PROJ_SPLIT:D_IN] = _dot(hb_ref[...], w_in_ref[:, PROJ_SPLIT:D_IN])
            group_c(z_ref, o_ref, (t_idx == 0) if sub == 0 else None)

        @pl.when(always())
        def _seg_out(sub=sub, o_ref=o_ref):
            out_proj(sub, o_ref)

    pl.when(t_idx == NT - 1)(functools.partial(new_states, z_refs[NSUB - 1]))


def _const_spec(shape):
    nd = len(shape)
    return pl.BlockSpec(shape, lambda *_, _nd=nd: (0,) * _nd, pipeline_mode=pl.Buffered(1))


def _layer_spec(shape, layer):
    nd = len(shape) - 1
    return pl.BlockSpec((None,) + tuple(shape[1:]), lambda *_, _nd=nd: (layer,) + (0,) * _nd,
                        pipeline_mode=pl.Buffered(1))


def _mixer_call(x, states, layer, lp, tabs, *, G, TB, CR, QS, NSUB, ln_in):
    B, T, _ = x.shape
    NS, NT = B // G, T // (NSUB * TB)
    has_past = states is not None
    assert NSUB == 1 or G == 1
    KN = TB if has_past else WINDOW
    cfg = (G, TB, CR, QS, NT, NSUB, has_past, ln_in)

    in_specs = [pl.BlockSpec((G, NSUB * TB, D_MODEL), lambda s, t: (s, t, 0))]
    args = [x]
    if NSUB > 1:
        last_block = T // TB - 1
        in_specs.append(pl.BlockSpec((G, TB, D_MODEL), lambda s, t: (s, jnp.minimum(NSUB * (t + 1), last_block), 0)))
        args.append(x)
    if has_past:
        for st in states:
            nd = st.ndim
            in_specs.append(pl.BlockSpec((None, G) + st.shape[2:],
                                         lambda s, t, _nd=nd: (layer, s) + (0,) * (_nd - 2)))
            args.append(st)
    for cst, stacked in lp:
        in_specs.append(_layer_spec(cst.shape, layer) if stacked else _const_spec(cst.shape))
        args.append(cst)
    for cst in tabs:
        in_specs.append(_const_spec(cst.shape))
        args.append(cst)

    out_shape = (
        jax.ShapeDtypeStruct((B, T, D_MODEL), F32),
        jax.ShapeDtypeStruct((B, CONV_W - 1, A_WIDTH), F32),
        jax.ShapeDtypeStruct((B, 1, A_WIDTH), F32),
        jax.ShapeDtypeStruct((B, B_HEADS, HEAD_DIM, HEAD_DIM), F32),
        jax.ShapeDtypeStruct((B, KN, C_KV_WIDTH), F32),
        jax.ShapeDtypeStruct((B, KN, C_KV_WIDTH), F32),
    )
    out_specs = (
        pl.BlockSpec((G, NSUB * TB, D_MODEL), lambda s, t: (s, t, 0)),
        pl.BlockSpec((G, CONV_W - 1, A_WIDTH), lambda s, t: (s, 0, 0)),
        pl.BlockSpec((G, 1, A_WIDTH), lambda s, t: (s, 0, 0)),
        pl.BlockSpec((G, B_HEADS, HEAD_DIM, HEAD_DIM), lambda s, t: (s, 0, 0, 0)),
        pl.BlockSpec((G, KN, C_KV_WIDTH), lambda s, t: (s, 0, 0)),
        pl.BlockSpec((G, KN, C_KV_WIDTH), lambda s, t: (s, 0, 0)),
    )
    R = G * TB
    scratch = [pltpu.VMEM((R, D_IN), F32) for _ in range(NSUB)]
    scratch += [
        pltpu.VMEM((NSUB, R, D_MIX), F32),
        pltpu.VMEM((NSUB, R, A_WIDTH), F32),
        pltpu.VMEM((G, SUBLANES + TB, A_WIDTH), F32),
        pltpu.VMEM((G, 1, A_WIDTH), F32),
        pltpu.VMEM((G, WINDOW + TB, C_KV_WIDTH), BF16),
        pltpu.VMEM((G, WINDOW + TB, C_KV_WIDTH), BF16),
        pltpu.VMEM((G, B_WIDTH, B_WIDTH), F32),
    ]
    if NSUB > 1:
        scratch.append(pltpu.VMEM((R, D_MODEL), BF16))
    return pl.pallas_call(
        functools.partial(_mixer_body, cfg),
        grid=(NS, NT),
        in_specs=in_specs,
        out_specs=out_specs,
        out_shape=out_shape,
        scratch_shapes=scratch,
        compiler_params=pltpu.CompilerParams(
            dimension_semantics=("arbitrary", "arbitrary"),
            vmem_limit_bytes=VMEM_LIMIT_BYTES),
        name="mixer_past" if has_past else "mixer_prompt",
    )(*args)


MXU_TILE = 256
FFN_BOUNDS = (0, 6 * MXU_TILE, D_FF)


def _ffn_body(h_ref, wg_ref, wu_ref, wd_ref, g_ref, b_ref, out_ref):
    h = h_ref[...]
    hb = h.astype(BF16)
    acc = jnp.zeros(h.shape, F32)
    for lo, hi in zip(FFN_BOUNDS[:-1], FFN_BOUNDS[1:]):
        cols = slice(lo, hi)
        a = _dot(hb, wg_ref[:, cols])
        b = _dot(hb, wu_ref[:, cols])
        act = (a * jax.nn.sigmoid(a) * b).astype(BF16)
        acc = acc + _dot(act, wd_ref[cols, :])
    out_ref[...] = _ln(ALPHA * h + acc, g_ref[...], b_ref[...])


def _ffn_call(h2d, layer, wg, wu, wd, g, b, *, TM):
    N = h2d.shape[0]
    return pl.pallas_call(
        _ffn_body,
        grid=(N // TM,),
        in_specs=[pl.BlockSpec((TM, D_MODEL), lambda i: (i, 0))]
                 + [_layer_spec(w.shape, layer) for w in (wg, wu, wd, g, b)],
        out_specs=pl.BlockSpec((TM, D_MODEL), lambda i: (i, 0)),
        out_shape=jax.ShapeDtypeStruct((N, D_MODEL), F32),
        compiler_params=pltpu.CompilerParams(
            dimension_semantics=("arbitrary",),
            vmem_limit_bytes=VMEM_LIMIT_BYTES),
        name="ffn",
    )(h2d, wg, wu, wd, g, b)


def _block_diag(w):
    nl, hh = w.shape[0], w.shape[1]
    eye = jnp.eye(hh, dtype=w.dtype)
    return (eye[None, :, None, :, None] * w[:, :, :, None, :]).reshape(nl, hh * HEAD_DIM, hh * HEAD_DIM)


def kernel(x_prompt, x_sample, state_conv, state_lru, state_ret, cache_k, cache_v, ln_in_g, ln_in_b, w_in, conv_w, conv_b, w_rg_a, b_rg_a, w_rg_x, b_rg_x, lru_lambda, ret_gn_g, ret_gn_b, sinks, w_out, ln1_g, ln1_b, w_gate, w_up, w_down, ln2_g, ln2_b):
    nb_p, nb_s = x_prompt.shape[0], x_sample.shape[0]
    wc = cache_k.shape[2]
    rows = lambda v: v.reshape(DEPTH, 1, -1)
    w_in_b = w_in.astype(BF16)
    w_out_b = w_out.astype(BF16)
    w_gate_b = w_gate.astype(BF16)
    w_up_b = w_up.astype(BF16)
    w_down_b = w_down.astype(BF16)
    ln2_gr, ln2_br = rows(ln2_g), rows(ln2_b)

    PROMPT = dict(G=1, TB=1024, CR=256, QS=128, NSUB=1)
    SAMPLE = dict(G=8, TB=32, CR=32, QS=32, NSUB=1)
    tabs_p = _retention_tables(PROMPT["CR"]) + (_alibi_table(PROMPT["QS"]),)
    tabs_s = _retention_tables(SAMPLE["CR"]) + (_alibi_table(SAMPLE["QS"]),)

    w_gates = jnp.concatenate([_block_diag(w_rg_a), _block_diag(w_rg_x)], axis=2).astype(BF16)
    b_gates = jnp.concatenate([b_rg_a, b_rg_x], axis=1).reshape(DEPTH, 1, 2 * A_WIDTH)
    shared = [(w_in_b, True), (conv_w, True), (rows(conv_b), True), (w_gates, True), (b_gates, True),
              (rows(lru_lambda), True), (rows(ret_gn_g), True), (rows(ret_gn_b), True)]
    tail = [(w_out_b, True), (rows(ln1_g), True), (rows(ln1_b), True)]
    head = [(ln_in_g.reshape(1, -1), False), (ln_in_b.reshape(1, -1), False)]

    def mixer_params(qs):
        sink_rows = jnp.repeat(sinks * LOG2E, qs, axis=1).reshape(DEPTH, C_KV_HEADS, 1, C_GROUP * qs)
        return head + shared + [(sink_rows, True)] + tail

    lp_p, lp_s = mixer_params(PROMPT["QS"]), mixer_params(SAMPLE["QS"])
    states = (state_conv, state_lru.reshape(DEPTH, nb_s, 1, A_WIDTH), state_ret,
              cache_k.reshape(DEPTH, nb_s, wc, C_KV_WIDTH), cache_v.reshape(DEPTH, nb_s, wc, C_KV_WIDTH))

    hp, hs = x_prompt, x_sample
    outs_p, outs_s = [], []
    for l in range(DEPTH):
        res = _mixer_call(hp, None, l, lp_p, tabs_p, ln_in=(l == 0), **PROMPT)
        hp, st = res[0], res[1:]
        outs_p.append(st)
        hp = _ffn_call(hp.reshape(-1, D_MODEL), l, w_gate_b, w_up_b, w_down_b, ln2_gr, ln2_br,
                       TM=512).reshape(x_prompt.shape)
        res = _mixer_call(hs, states, l, lp_s, tabs_s, ln_in=(l == 0), **SAMPLE)
        hs, st = res[0], res[1:]
        outs_s.append(st)
        hs = _ffn_call(hs.reshape(-1, D_MODEL), l, w_gate_b, w_up_b, w_down_b, ln2_gr, ln2_br,
                       TM=512).reshape(x_sample.shape)

    def stack(outs, nb, kn):
        conv = jnp.stack([o[0] for o in outs])
        lru = jnp.stack([o[1].reshape(nb, A_WIDTH) for o in outs])
        ret = jnp.stack([o[2] for o in outs])
        kk = jnp.stack([o[3].reshape(nb, kn, C_KV_HEADS, HEAD_DIM) for o in outs])
        vv = jnp.stack([o[4].reshape(nb, kn, C_KV_HEADS, HEAD_DIM) for o in outs])
        return conv, lru, ret, kk, vv

    p_conv, p_lru, p_ret, p_k, p_v = stack(outs_p, nb_p, WINDOW)
    s_conv, s_lru, s_ret, s_k, s_v = stack(outs_s, nb_s, SAMPLE["TB"])
    return (hp, hs, p_conv, p_lru, p_ret, p_k, p_v, s_conv, s_lru, s_ret, s_k, s_v)
```

```python
import functools
import math

import numpy as np
import jax
import jax.numpy as jnp
from jax import lax
from jax.experimental import pallas as pl
from jax.experimental.pallas import tpu as pltpu

F32 = jnp.float32
BF16 = jnp.bfloat16

D_MODEL = 1024
DEPTH = 4
HEAD_DIM = 64
A_WIDTH = 256
B_HEADS = 4
B_WIDTH = 256
C_HEADS = 8
C_KV_HEADS = 2
C_GROUP = 4
C_WIDTH = 512
C_KV_WIDTH = 128
CONV_W = 4
LRU_C = 8.0
WINDOW = 128
ATTN_CHUNK = 64
D_MIX = 1024
D_IN = 2304
D_FF = 2816
ALPHA = (2.0 * DEPTH) ** 0.25
LN_EPS = 1e-5
NEG = -1e30
LOG2E = math.log2(math.e)

OFF_AX, OFF_AG = 0, 256
OFF_BQ, OFF_BK, OFF_BV, OFF_BG = 512, 768, 1024, 1280
OFF_CQ, OFF_CK, OFF_CV = 1536, 2048, 2176

SUBLANES = 8
MXU_TILE = 256
VMEM_LIMIT_BYTES = 52 * 1024 * 1024


def _ln(x, g, b):
    mu = jnp.mean(x, axis=-1, keepdims=True)
    d = x - mu
    var = jnp.mean(d * d, axis=-1, keepdims=True)
    return d * lax.rsqrt(var + LN_EPS) * g + b


def _dot(a, b):
    return jnp.dot(a, b, preferred_element_type=F32)


def _dot_nt(a, b):
    return lax.dot_general(a, b, (((1,), (1,)), ((), ())), preferred_element_type=F32)


def _dot_tn(a, b):
    return lax.dot_general(a, b, (((0,), (0,)), ((), ())), preferred_element_type=F32)


def _retention_tables(chunk):
    lg = np.log1p(-(2.0 ** (-5.0 - np.arange(B_HEADS, dtype=np.float64))))
    idx = np.arange(chunk, dtype=np.float64)
    diff = idx[:, None] - idx[None, :]
    dmat = np.where(diff >= 0, np.exp(np.maximum(diff, 0.0)[None] * lg[:, None, None]), 0.0)
    dmat = dmat * HEAD_DIM ** -0.5
    dmat_all = np.concatenate([dmat[h] for h in range(B_HEADS)], axis=1)
    w_start = np.exp((idx + 1.0)[:, None] * lg[None, :])
    w_end = np.exp((chunk - 1.0 - idx)[:, None] * lg[None, :]) * HEAD_DIM ** -0.5
    w_start = np.repeat(w_start, HEAD_DIM, axis=1)
    w_end = np.repeat(w_end, HEAD_DIM, axis=1)
    g_chunk = np.repeat(np.exp(chunk * lg), HEAD_DIM)
    head = np.arange(B_WIDTH) // HEAD_DIM
    blk = (head[:, None] == head[None, :]).astype(np.float64)
    g_bd = blk * g_chunk[:, None]
    avg_bd = blk / HEAD_DIM
    return (jnp.asarray(dmat_all, F32), jnp.asarray(w_start, F32), jnp.asarray(w_end, F32),
            jnp.asarray(g_bd, F32), jnp.asarray(blk, F32), jnp.asarray(avg_bd, BF16))


def _alibi_table(qs):
    w = WINDOW + qs
    slopes = 2.0 ** (-8.0 * np.arange(1, C_HEADS + 1, dtype=np.float64) / C_HEADS)
    qpos = np.arange(qs)
    kpos = np.arange(w) - WINDOW
    dist = np.abs(qpos[:, None] - kpos[None, :]).astype(np.float64)
    q_chunk = qpos // ATTN_CHUNK
    k_chunk = np.floor_divide(kpos, ATTN_CHUNK)
    visible = (k_chunk[None, :] <= q_chunk[:, None]) & (k_chunk[None, :] >= q_chunk[:, None] - WINDOW // ATTN_CHUNK)
    tab = slopes[:, None, None] * dist[None]
    tab = np.where(visible[None], tab * LOG2E, -NEG)
    tab = tab.reshape(C_KV_HEADS, C_GROUP * qs, w).transpose(0, 2, 1)
    return jnp.asarray(tab, F32)


def _mixer_body(cfg, *refs):
    G, TB, CR, QS, NT, has_past, ln_in = cfg
    R = G * TB
    W = WINDOW + QS
    carried = NT > 1
    it = iter(refs)
    x_ref = next(it)
    if has_past:
        sconv_ref, slru_ref, sret_ref, ck_ref, cv_ref = (next(it) for _ in range(5))
    (ln_in_g, ln_in_b, w_in_ref, conv_w_ref, conv_b_ref, w_gate_ref, b_gate_ref, lam_ref,
     gn_g_ref, gn_b_ref, sink_ref, w_out_ref, ln1_g, ln1_b,
     dmat_ref, wstart_ref, wend_ref, gbd_ref, blk_ref, avg_ref, alibi_ref) = (next(it) for _ in range(21))
    h1_ref, oconv_ref, olru_ref, oret_ref, ok_ref, ov_ref = (next(it) for _ in range(6))
    z_ref, o_ref, sb_ref, axbuf, hc_ref, kwin, vwin, sbd = (next(it) for _ in range(8))

    t_idx = pl.program_id(1)

    @pl.when(t_idx == 0)
    def _init():
        for g in range(G):
            if has_past:
                axbuf[g, SUBLANES - 3:SUBLANES, :] = sconv_ref[g]
                hc_ref[g] = slru_ref[g]
                sbd[g] = jnp.zeros((B_WIDTH, B_WIDTH), F32)
                for hd in range(B_HEADS):
                    sl = slice(hd * HEAD_DIM, (hd + 1) * HEAD_DIM)
                    sbd[g, sl, sl] = sret_ref[g, hd]
                kwin[g, 0:WINDOW, :] = ck_ref[g].astype(BF16)
                vwin[g, 0:WINDOW, :] = cv_ref[g].astype(BF16)
            else:
                axbuf[g, 0:SUBLANES, :] = jnp.zeros((SUBLANES, A_WIDTH), F32)
                hc_ref[g] = jnp.zeros((1, A_WIDTH), F32)
                sbd[g] = jnp.zeros((B_WIDTH, B_WIDTH), F32)
                kwin[g, 0:WINDOW, :] = jnp.zeros((WINDOW, C_KV_WIDTH), BF16)
                vwin[g, 0:WINDOW, :] = jnp.zeros((WINDOW, C_KV_WIDTH), BF16)

    def group_a():
        nl = -lam_ref[...]
        softplus_nl = jnp.maximum(nl, 0.0) + jnp.log1p(jnp.exp(-jnp.abs(nl)))
        cw = conv_w_ref[...]
        row_in_tile = lax.broadcasted_iota(jnp.int32, (TB // SUBLANES, SUBLANES, A_WIDTH), 1)
        for g in range(G):
            rows = slice(g * TB, (g + 1) * TB)
            ax = z_ref[rows, OFF_AX:OFF_AX + A_WIDTH]
            axbuf[g, SUBLANES:SUBLANES + TB, :] = ax
            u = conv_b_ref[...] + axbuf[g, SUBLANES - 3:SUBLANES - 3 + TB, :] * cw[0:1, :]
            u = u + axbuf[g, SUBLANES - 2:SUBLANES - 2 + TB, :] * cw[1:2, :]
            u = u + axbuf[g, SUBLANES - 1:SUBLANES - 1 + TB, :] * cw[2:3, :]
            u = u + ax * cw[3:4, :]
            if carried:
                axbuf[g, SUBLANES - 3:SUBLANES, :] = axbuf[g, SUBLANES + TB - 3:SUBLANES + TB, :]
            gz = _dot(u.astype(BF16), w_gate_ref[...]) + b_gate_ref[...]
            r = jax.nn.sigmoid(gz[:, 0:A_WIDTH])
            gi = jax.nn.sigmoid(gz[:, A_WIDTH:2 * A_WIDTH])
            log_a = (-LRU_C) * r * softplus_nl
            a = jnp.exp(log_a)
            th = jnp.tanh(log_a)
            one_minus_a2 = (-2.0) * th / (1.0 - th)
            bt = jnp.sqrt(one_minus_a2) * (gi * u)
            a = a.reshape(TB // SUBLANES, SUBLANES, A_WIDTH)
            bt = bt.reshape(TB // SUBLANES, SUBLANES, A_WIDTH)
            for s in (1, 2, 4):
                keep = row_in_tile >= s
                bt = jnp.where(keep, a * pltpu.roll(bt, s, 1) + bt, bt)
                a = jnp.where(keep, a * pltpu.roll(a, s, 1), a)
            hprev = jnp.broadcast_to(hc_ref[g], (SUBLANES, A_WIDTH))
            for i in range(TB // SUBLANES):
                ht = a[i] * hprev + bt[i]
                sb_ref[g * TB + i * SUBLANES:g * TB + (i + 1) * SUBLANES, :] = ht
                hprev = jnp.broadcast_to(ht[SUBLANES - 1:SUBLANES, :], (SUBLANES, A_WIDTH))
            hc_ref[g] = hprev[0:1, :]
            ha = sb_ref[rows, :]
            ag = z_ref[rows, OFF_AG:OFF_AG + A_WIDTH]
            o_ref[rows, 0:A_WIDTH] = ha * jax.nn.gelu(ag)

    def group_b():
        lane_head = lax.broadcasted_iota(jnp.int32, (CR, B_WIDTH), 1) // HEAD_DIM
        for g in range(G):
            for c in range(TB // CR):
                rows = slice(g * TB + c * CR, g * TB + (c + 1) * CR)
                q = z_ref[rows, OFF_BQ:OFF_BQ + B_WIDTH]
                k = z_ref[rows, OFF_BK:OFF_BK + B_WIDTH]
                v = z_ref[rows, OFF_BV:OFF_BV + B_WIDTH]
                kx = jnp.concatenate([jnp.where(lane_head == hh, k, 0.0) for hh in range(B_HEADS)], axis=0).astype(BF16)
                vx = jnp.concatenate([jnp.where(lane_head == hh, v, 0.0) for hh in range(B_HEADS)], axis=0).astype(BF16)
                inner = _dot_nt(q.astype(BF16), kx) * dmat_ref[...]
                s_prev = sbd[g]
                ro = _dot(inner.astype(BF16), vx) + _dot((q * wstart_ref[...]).astype(BF16), s_prev.astype(BF16))
                upd = _dot_tn((k * wend_ref[...]).astype(BF16), v.astype(BF16))
                sbd[g] = gbd_ref[...] * s_prev + upd * blk_ref[...]
                o_ref[rows, A_WIDTH:A_WIDTH + B_WIDTH] = ro
        ro = o_ref[:, A_WIDTH:A_WIDTH + B_WIDTH]
        bgate = z_ref[:, OFF_BG:OFF_BG + B_WIDTH]
        mu = _dot(ro.astype(BF16), avg_ref[...])
        dlt = ro - mu
        var = _dot((dlt * dlt).astype(BF16), avg_ref[...])
        on = dlt * lax.rsqrt(var + LN_EPS) * gn_g_ref[...] + gn_b_ref[...]
        o_ref[:, A_WIDTH:A_WIDTH + B_WIDTH] = on * (bgate * jax.nn.sigmoid(bgate))

    def group_c():
        for g in range(G):
            rows_g = slice(g * TB, (g + 1) * TB)
            kwin[g, WINDOW:WINDOW + TB, :] = z_ref[rows_g, OFF_CK:OFF_CK + C_KV_WIDTH].astype(BF16)
            vwin[g, WINDOW:WINDOW + TB, :] = z_ref[rows_g, OFF_CV:OFF_CV + C_KV_WIDTH].astype(BF16)
            for c in range(TB // QS):
                rows = slice(g * TB + c * QS, g * TB + (c + 1) * QS)
                kw = kwin[g, c * QS:c * QS + W, :]
                vw = vwin[g, c * QS:c * QS + W, :]
                for kvh in range(C_KV_HEADS):
                    qs = jnp.concatenate(
                        [z_ref[rows, OFF_CQ + (kvh * C_GROUP + j) * HEAD_DIM:OFF_CQ + (kvh * C_GROUP + j + 1) * HEAD_DIM]
                         for j in range(C_GROUP)], axis=0)
                    qs = (qs * (HEAD_DIM ** -0.5 * LOG2E)).astype(BF16)
                    s = _dot_nt(kw[:, kvh * HEAD_DIM:(kvh + 1) * HEAD_DIM], qs) - alibi_ref[kvh]
                    if (not has_past) and c == 0:
                        key_id = lax.broadcasted_iota(jnp.int32, (W, C_GROUP * QS), 0)
                        n_invalid = jnp.where(t_idx == 0, WINDOW, 0)
                        s = jnp.where(key_id >= n_invalid, s, NEG)
                    sk = sink_ref[kvh]
                    m = jnp.maximum(jnp.max(s, axis=0, keepdims=True), sk)
                    e = jnp.exp2(s - m)
                    den = jnp.sum(e, axis=0, keepdims=True) + jnp.exp2(sk - m)
                    ot = _dot_tn(vw, e.astype(BF16))
                    ot = ot[kvh * HEAD_DIM:(kvh + 1) * HEAD_DIM, :] * (1.0 / den)
                    oc = ot.T
                    oc = jnp.concatenate([oc[j * QS:(j + 1) * QS, :] for j in range(C_GROUP)], axis=1)
                    lo = A_WIDTH + B_WIDTH + kvh * C_GROUP * HEAD_DIM
                    o_ref[rows, lo:lo + C_GROUP * HEAD_DIM] = oc
            if carried:
                kwin[g, 0:WINDOW, :] = kwin[g, TB:TB + WINDOW, :]
                vwin[g, 0:WINDOW, :] = vwin[g, TB:TB + WINDOW, :]

    def new_states():
        kn = ok_ref.shape[1]
        for g in range(G):
            oconv_ref[g] = axbuf[g, SUBLANES + TB - 3:SUBLANES + TB, :]
            olru_ref[g] = hc_ref[g]
            for hd in range(B_HEADS):
                sl = slice(hd * HEAD_DIM, (hd + 1) * HEAD_DIM)
                oret_ref[g, hd] = sbd[g, sl, sl]
            last = slice((g + 1) * TB - kn, (g + 1) * TB)
            ok_ref[g] = z_ref[last, OFF_CK:OFF_CK + C_KV_WIDTH]
            ov_ref[g] = z_ref[last, OFF_CV:OFF_CV + C_KV_WIDTH]

    x = x_ref[...].reshape(R, D_MODEL)
    h = _ln(x, ln_in_g[...], ln_in_b[...]) if ln_in else x
    z_ref[...] = _dot(h.astype(BF16), w_in_ref[...])
    group_a()
    group_b()
    group_c()
    y = _dot(o_ref[...].astype(BF16), w_out_ref[...])
    h1_ref[...] = _ln(ALPHA * h + y, ln1_g[...], ln1_b[...]).reshape(G, TB, D_MODEL)
    pl.when(t_idx == NT - 1)(new_states)


def _const_spec(shape):
    nd = len(shape)
    return pl.BlockSpec(shape, lambda *_, _nd=nd: (0,) * _nd, pipeline_mode=pl.Buffered(1))


def _layer_spec(shape, layer):
    nd = len(shape) - 1
    return pl.BlockSpec((None,) + tuple(shape[1:]), lambda *_, _nd=nd: (layer,) + (0,) * _nd,
                        pipeline_mode=pl.Buffered(1))


def _mixer_call(x, states, layer, lp, tabs, *, G, TB, CR, QS, ln_in):
    B, T, _ = x.shape
    NS, NT = B // G, T // TB
    has_past = states is not None
    KN = TB if has_past else WINDOW
    cfg = (G, TB, CR, QS, NT, has_past, ln_in)

    in_specs = [pl.BlockSpec((G, TB, D_MODEL), lambda s, t: (s, t, 0))]
    args = [x]
    if has_past:
        for st in states:
            nd = st.ndim
            in_specs.append(pl.BlockSpec((None, G) + st.shape[2:],
                                         lambda s, t, _nd=nd: (layer, s) + (0,) * (_nd - 2)))
            args.append(st)
    for cst, stacked in lp:
        in_specs.append(_layer_spec(cst.shape, layer) if stacked else _const_spec(cst.shape))
        args.append(cst)
    for cst in tabs:
        in_specs.append(_const_spec(cst.shape))
        args.append(cst)

    out_shape = (
        jax.ShapeDtypeStruct((B, T, D_MODEL), F32),
        jax.ShapeDtypeStruct((B, CONV_W - 1, A_WIDTH), F32),
        jax.ShapeDtypeStruct((B, 1, A_WIDTH), F32),
        jax.ShapeDtypeStruct((B, B_HEADS, HEAD_DIM, HEAD_DIM), F32),
        jax.ShapeDtypeStruct((B, KN, C_KV_WIDTH), F32),
        jax.ShapeDtypeStruct((B, KN, C_KV_WIDTH), F32),
    )
    out_specs = (
        pl.BlockSpec((G, TB, D_MODEL), lambda s, t: (s, t, 0)),
        pl.BlockSpec((G, CONV_W - 1, A_WIDTH), lambda s, t: (s, 0, 0)),
        pl.BlockSpec((G, 1, A_WIDTH), lambda s, t: (s, 0, 0)),
        pl.BlockSpec((G, B_HEADS, HEAD_DIM, HEAD_DIM), lambda s, t: (s, 0, 0, 0)),
        pl.BlockSpec((G, KN, C_KV_WIDTH), lambda s, t: (s, 0, 0)),
        pl.BlockSpec((G, KN, C_KV_WIDTH), lambda s, t: (s, 0, 0)),
    )
    R = G * TB
    scratch = [
        pltpu.VMEM((R, D_IN), F32),
        pltpu.VMEM((R, D_MIX), F32),
        pltpu.VMEM((R, A_WIDTH), F32),
        pltpu.VMEM((G, SUBLANES + TB, A_WIDTH), F32),
        pltpu.VMEM((G, 1, A_WIDTH), F32),
        pltpu.VMEM((G, WINDOW + TB, C_KV_WIDTH), BF16),
        pltpu.VMEM((G, WINDOW + TB, C_KV_WIDTH), BF16),
        pltpu.VMEM((G, B_WIDTH, B_WIDTH), F32),
    ]
    return pl.pallas_call(
        functools.partial(_mixer_body, cfg),
        grid=(NS, NT),
        in_specs=in_specs,
        out_specs=out_specs,
        out_shape=out_shape,
        scratch_shapes=scratch,
        compiler_params=pltpu.CompilerParams(
            dimension_semantics=("arbitrary", "arbitrary"),
            vmem_limit_bytes=VMEM_LIMIT_BYTES),
        name="mixer_past" if has_past else "mixer_prompt",
    )(*args)


FFN_BOUNDS = (0, 6 * MXU_TILE, D_FF)
FFN_ROWS = 512


def _ffn_body(h_ref, wg_ref, wu_ref, wd_ref, g_ref, b_ref, out_ref):
    for r0 in range(0, h_ref.shape[0], FFN_ROWS):
        rows = slice(r0, r0 + FFN_ROWS)
        h = h_ref[rows, :]
        hb = h.astype(BF16)
        acc = jnp.zeros(h.shape, F32)
        for lo, hi in zip(FFN_BOUNDS[:-1], FFN_BOUNDS[1:]):
            cols = slice(lo, hi)
            a = _dot(hb, wg_ref[:, cols])
            b = _dot(hb, wu_ref[:, cols])
            act = (a * jax.nn.sigmoid(a) * b).astype(BF16)
            acc = acc + _dot(act, wd_ref[cols, :])
        out_ref[rows, :] = _ln(ALPHA * h + acc, g_ref[...], b_ref[...])


def _ffn_call(h2d, layer, wg, wu, wd, g, b, *, TM):
    N = h2d.shape[0]
    return pl.pallas_call(
        _ffn_body,
        grid=(N // TM,),
        in_specs=[pl.BlockSpec((TM, D_MODEL), lambda i: (i, 0))]
                 + [_layer_spec(w.shape, layer) for w in (wg, wu, wd, g, b)],
        out_specs=pl.BlockSpec((TM, D_MODEL), lambda i: (i, 0)),
        out_shape=jax.ShapeDtypeStruct((N, D_MODEL), F32),
        compiler_params=pltpu.CompilerParams(
            dimension_semantics=("arbitrary",),
            vmem_limit_bytes=VMEM_LIMIT_BYTES),
        name="ffn",
    )(h2d, wg, wu, wd, g, b)


def _block_diag(w):
    nl, hh = w.shape[0], w.shape[1]
    eye = jnp.eye(hh, dtype=w.dtype)
    return (eye[None, :, None, :, None] * w[:, :, :, None, :]).reshape(nl, hh * HEAD_DIM, hh * HEAD_DIM)


def kernel(x_prompt, x_sample, state_conv, state_lru, state_ret, cache_k, cache_v, ln_in_g, ln_in_b, w_in, conv_w, conv_b, w_rg_a, b_rg_a, w_rg_x, b_rg_x, lru_lambda, ret_gn_g, ret_gn_b, sinks, w_out, ln1_g, ln1_b, w_gate, w_up, w_down, ln2_g, ln2_b):
    nb_p, nb_s = x_prompt.shape[0], x_sample.shape[0]
    wc = cache_k.shape[2]
    rows = lambda v: v.reshape(DEPTH, 1, -1)
    w_in_b = w_in.astype(BF16)
    w_out_b = w_out.astype(BF16)
    w_gate_b = w_gate.astype(BF16)
    w_up_b = w_up.astype(BF16)
    w_down_b = w_down.astype(BF16)
    ln2_gr, ln2_br = rows(ln2_g), rows(ln2_b)

    PROMPT = dict(G=1, TB=1024, CR=256, QS=128)
    SAMPLE = dict(G=8, TB=32, CR=32, QS=32)
    FFN_TM = 1024
    tabs_p = _retention_tables(PROMPT["CR"]) + (_alibi_table(PROMPT["QS"]),)
    tabs_s = _retention_tables(SAMPLE["CR"]) + (_alibi_table(SAMPLE["QS"]),)

    w_gates = jnp.concatenate([_block_diag(w_rg_a), _block_diag(w_rg_x)], axis=2).astype(BF16)
    b_gates = jnp.concatenate([b_rg_a, b_rg_x], axis=1).reshape(DEPTH, 1, 2 * A_WIDTH)
    shared = [(w_in_b, True), (conv_w, True), (rows(conv_b), True), (w_gates, True), (b_gates, True),
              (rows(lru_lambda), True), (rows(ret_gn_g), True), (rows(ret_gn_b), True)]
    tail = [(w_out_b, True), (rows(ln1_g), True), (rows(ln1_b), True)]
    head = [(ln_in_g.reshape(1, -1), False), (ln_in_b.reshape(1, -1), False)]

    def mixer_params(qs):
        sink_rows = jnp.repeat(sinks * LOG2E, qs, axis=1).reshape(DEPTH, C_KV_HEADS, 1, C_GROUP * qs)
        return head + shared + [(sink_rows, True)] + tail

    lp_p, lp_s = mixer_params(PROMPT["QS"]), mixer_params(SAMPLE["QS"])
    states = (state_conv, state_lru.reshape(DEPTH, nb_s, 1, A_WIDTH), state_ret,
              cache_k.reshape(DEPTH, nb_s, wc, C_KV_WIDTH), cache_v.reshape(DEPTH, nb_s, wc, C_KV_WIDTH))

    hp, hs = x_prompt, x_sample
    outs_p, outs_s = [], []
    for l in range(DEPTH):
        res = _mixer_call(hp, None, l, lp_p, tabs_p, ln_in=(l == 0), **PROMPT)
        hp, st = res[0], res[1:]
        outs_p.append(st)
        hp = _ffn_call(hp.reshape(-1, D_MODEL), l, w_gate_b, w_up_b, w_down_b, ln2_gr, ln2_br,
                       TM=FFN_TM).reshape(x_prompt.shape)
        res = _mixer_call(hs, states, l, lp_s, tabs_s, ln_in=(l == 0), **SAMPLE)
        hs, st = res[0], res[1:]
        outs_s.append(st)
        hs = _ffn_call(hs.reshape(-1, D_MODEL), l, w_gate_b, w_up_b, w_down_b, ln2_gr, ln2_br,
                       TM=FFN_TM).reshape(x_sample.shape)

    def stack(outs, nb, kn):
        conv = jnp.stack([o[0] for o in outs])
        lru = jnp.stack([o[1].reshape(nb, A_WIDTH) for o in outs])
        ret = jnp.stack([o[2] for o in outs])
        kk = jnp.stack([o[3].reshape(nb, kn, C_KV_HEADS, HEAD_DIM) for o in outs])
        vv = jnp.stack([o[4].reshape(nb, kn, C_KV_HEADS, HEAD_DIM) for o in outs])
        return conv, lru, ret, kk, vv

    p_conv, p_lru, p_ret, p_k, p_v = stack(outs_p, nb_p, WINDOW)
    s_conv, s_lru, s_ret, s_k, s_v = stack(outs_s, nb_s, SAMPLE["TB"])
    return (hp, hs, p_conv, p_lru, p_ret, p_k, p_v, s_conv, s_lru, s_ret, s_k, s_v)
```

```python
import functools
import math

import numpy as np
import jax
import jax.numpy as jnp
from jax import lax
from jax.experimental import pallas as pl
from jax.experimental.pallas import tpu as pltpu

F32 = jnp.float32
BF16 = jnp.bfloat16

D_MODEL = 1024
DEPTH = 4
HEAD_DIM = 64
A_WIDTH = 256
B_HEADS = 4
B_WIDTH = 256
C_HEADS = 8
C_KV_HEADS = 2
C_GROUP = 4
C_WIDTH = 512
C_KV_WIDTH = 128
CONV_W = 4
LRU_C = 8.0
WINDOW = 128
ATTN_CHUNK = 64
D_MIX = 1024
D_IN = 2304
D_FF = 2816
ALPHA = (2.0 * DEPTH) ** 0.25
LN_EPS = 1e-5
NEG = -1e30
LOG2E = math.log2(math.e)

OFF_AX, OFF_AG = 0, 256
OFF_BQ, OFF_BK, OFF_BV, OFF_BG = 512, 768, 1024, 1280
OFF_CQ, OFF_CK, OFF_CV = 1536, 2048, 2176

SUBLANES = 8
MXU_TILE = 256
VMEM_LIMIT_BYTES = 52 * 1024 * 1024


def _ln(x, g, b):
    mu = jnp.mean(x, axis=-1, keepdims=True)
    d = x - mu
    var = jnp.mean(d * d, axis=-1, keepdims=True)
    return d * lax.rsqrt(var + LN_EPS) * g + b


def _dot(a, b):
    return jnp.dot(a, b, preferred_element_type=F32)


def _dot_nt(a, b):
    return lax.dot_general(a, b, (((1,), (1,)), ((), ())), preferred_element_type=F32)


def _dot_tn(a, b):
    return lax.dot_general(a, b, (((0,), (0,)), ((), ())), preferred_element_type=F32)


def _retention_tables(chunk):
    lg = np.log1p(-(2.0 ** (-5.0 - np.arange(B_HEADS, dtype=np.float64))))
    idx = np.arange(chunk, dtype=np.float64)
    diff = idx[:, None] - idx[None, :]
    dmat = np.where(diff >= 0, np.exp(np.maximum(diff, 0.0)[None] * lg[:, None, None]), 0.0)
    dmat = dmat * HEAD_DIM ** -0.5
    dmat_all = np.concatenate([dmat[h] for h in range(B_HEADS)], axis=1)
    w_start = np.exp((idx + 1.0)[:, None] * lg[None, :])
    w_end = np.exp((chunk - 1.0 - idx)[:, None] * lg[None, :]) * HEAD_DIM ** -0.5
    w_start = np.repeat(w_start, HEAD_DIM, axis=1)
    w_end = np.repeat(w_end, HEAD_DIM, axis=1)
    g_chunk = np.repeat(np.exp(chunk * lg), HEAD_DIM)
    head = np.arange(B_WIDTH) // HEAD_DIM
    blk = (head[:, None] == head[None, :]).astype(np.float64)
    g_bd = blk * g_chunk[:, None]
    avg_bd = blk / HEAD_DIM
    return (jnp.asarray(dmat_all, F32), jnp.asarray(w_start, F32), jnp.asarray(w_end, F32),
            jnp.asarray(g_bd, F32), jnp.asarray(blk, F32), jnp.asarray(avg_bd, BF16))


def _alibi_table(qs):
    w = WINDOW + qs
    slopes = 2.0 ** (-8.0 * np.arange(1, C_HEADS + 1, dtype=np.float64) / C_HEADS)
    qpos = np.arange(qs)
    kpos = np.arange(w) - WINDOW
    dist = np.abs(qpos[:, None] - kpos[None, :]).astype(np.float64)
    q_chunk = qpos // ATTN_CHUNK
    k_chunk = np.floor_divide(kpos, ATTN_CHUNK)
    visible = (k_chunk[None, :] <= q_chunk[:, None]) & (k_chunk[None, :] >= q_chunk[:, None] - WINDOW // ATTN_CHUNK)
    tab = slopes[:, None, None] * dist[None]
    tab = np.where(visible[None], tab * LOG2E, -NEG)
    tab = tab.reshape(C_KV_HEADS, C_GROUP * qs, w).transpose(0, 2, 1)
    return jnp.asarray(tab, F32)


def _mixer_body(cfg, *refs):
    G, TB, CR, QS, NT, has_past, ln_in = cfg
    R = G * TB
    W = WINDOW + QS
    carried = NT > 1
    it = iter(refs)
    x_ref = next(it)
    if has_past:
        sconv_ref, slru_ref, sret_ref, ck_ref, cv_ref = (next(it) for _ in range(5))
    (ln_in_g, ln_in_b, w_in_ref, conv_w_ref, conv_b_ref, w_gate_ref, b_gate_ref, lam_ref,
     gn_g_ref, gn_b_ref, sink_ref, w_out_ref, ln1_g, ln1_b,
     dmat_ref, wstart_ref, wend_ref, gbd_ref, blk_ref, avg_ref, alibi_ref) = (next(it) for _ in range(21))
    h1_ref, oconv_ref, olru_ref, oret_ref, ok_ref, ov_ref = (next(it) for _ in range(6))
    z_ref, o_ref, sb_ref, axbuf, hc_ref, kwin, vwin, sbd = (next(it) for _ in range(8))

    t_idx = pl.program_id(1)

    @pl.when(t_idx == 0)
    def _init():
        for g in range(G):
            if has_past:
                axbuf[g, SUBLANES - 3:SUBLANES, :] = sconv_ref[g]
                hc_ref[g] = slru_ref[g]
                sbd[g] = jnp.zeros((B_WIDTH, B_WIDTH), F32)
                for hd in range(B_HEADS):
                    sl = slice(hd * HEAD_DIM, (hd + 1) * HEAD_DIM)
                    sbd[g, sl, sl] = sret_ref[g, hd]
                kwin[g, 0:WINDOW, :] = ck_ref[g].astype(BF16)
                vwin[g, 0:WINDOW, :] = cv_ref[g].astype(BF16)
            else:
                axbuf[g, 0:SUBLANES, :] = jnp.zeros((SUBLANES, A_WIDTH), F32)
                hc_ref[g] = jnp.zeros((1, A_WIDTH), F32)
                sbd[g] = jnp.zeros((B_WIDTH, B_WIDTH), F32)
                kwin[g, 0:WINDOW, :] = jnp.zeros((WINDOW, C_KV_WIDTH), BF16)
                vwin[g, 0:WINDOW, :] = jnp.zeros((WINDOW, C_KV_WIDTH), BF16)

    def group_a():
        nl = -lam_ref[...]
        softplus_nl = jnp.maximum(nl, 0.0) + jnp.log1p(jnp.exp(-jnp.abs(nl)))
        cw = conv_w_ref[...]
        row_in_tile = lax.broadcasted_iota(jnp.int32, (TB // SUBLANES, SUBLANES, A_WIDTH), 1)
        for g in range(G):
            rows = slice(g * TB, (g + 1) * TB)
            ax = z_ref[rows, OFF_AX:OFF_AX + A_WIDTH]
            axbuf[g, SUBLANES:SUBLANES + TB, :] = ax
            u = conv_b_ref[...] + axbuf[g, SUBLANES - 3:SUBLANES - 3 + TB, :] * cw[0:1, :]
            u = u + axbuf[g, SUBLANES - 2:SUBLANES - 2 + TB, :] * cw[1:2, :]
            u = u + axbuf[g, SUBLANES - 1:SUBLANES - 1 + TB, :] * cw[2:3, :]
            u = u + ax * cw[3:4, :]
            if carried:
                axbuf[g, SUBLANES - 3:SUBLANES, :] = axbuf[g, SUBLANES + TB - 3:SUBLANES + TB, :]
            gz = _dot(u.astype(BF16), w_gate_ref[...]) + b_gate_ref[...]
            r = jax.nn.sigmoid(gz[:, 0:A_WIDTH])
            gi = jax.nn.sigmoid(gz[:, A_WIDTH:2 * A_WIDTH])
            log_a = (-LRU_C) * r * softplus_nl
            a = jnp.exp(log_a)
            th = jnp.tanh(log_a)
            one_minus_a2 = (-2.0) * th / (1.0 - th)
            bt = jnp.sqrt(one_minus_a2) * (gi * u)
            a = a.reshape(TB // SUBLANES, SUBLANES, A_WIDTH)
            bt = bt.reshape(TB // SUBLANES, SUBLANES, A_WIDTH)
            for s in (1, 2, 4):
                keep = row_in_tile >= s
                bt = jnp.where(keep, a * pltpu.roll(bt, s, 1) + bt, bt)
                a = jnp.where(keep, a * pltpu.roll(a, s, 1), a)
            hprev = jnp.broadcast_to(hc_ref[g], (SUBLANES, A_WIDTH))
            for i in range(TB // SUBLANES):
                ht = a[i] * hprev + bt[i]
                sb_ref[g * TB + i * SUBLANES:g * TB + (i + 1) * SUBLANES, :] = ht
                hprev = jnp.broadcast_to(ht[SUBLANES - 1:SUBLANES, :], (SUBLANES, A_WIDTH))
            hc_ref[g] = hprev[0:1, :]
            ha = sb_ref[rows, :]
            ag = z_ref[rows, OFF_AG:OFF_AG + A_WIDTH]
            o_ref[rows, 0:A_WIDTH] = ha * jax.nn.gelu(ag)

    def group_b():
        lane_head = lax.broadcasted_iota(jnp.int32, (CR, B_WIDTH), 1) // HEAD_DIM
        for g in range(G):
            for c in range(TB // CR):
                rows = slice(g * TB + c * CR, g * TB + (c + 1) * CR)
                q = z_ref[rows, OFF_BQ:OFF_BQ + B_WIDTH]
                k = z_ref[rows, OFF_BK:OFF_BK + B_WIDTH]
                v = z_ref[rows, OFF_BV:OFF_BV + B_WIDTH]
                kx = jnp.concatenate([jnp.where(lane_head == hh, k, 0.0) for hh in range(B_HEADS)], axis=0).astype(BF16)
                vx = jnp.concatenate([jnp.where(lane_head == hh, v, 0.0) for hh in range(B_HEADS)], axis=0).astype(BF16)
                inner = _dot_nt(q.astype(BF16), kx) * dmat_ref[...]
                s_prev = sbd[g]
                ro = _dot(inner.astype(BF16), vx) + _dot((q * wstart_ref[...]).astype(BF16), s_prev.astype(BF16))
                upd = _dot_tn((k * wend_ref[...]).astype(BF16), v.astype(BF16))
                sbd[g] = gbd_ref[...] * s_prev + upd * blk_ref[...]
                o_ref[rows, A_WIDTH:A_WIDTH + B_WIDTH] = ro
        ro = o_ref[:, A_WIDTH:A_WIDTH + B_WIDTH]
        bgate = z_ref[:, OFF_BG:OFF_BG + B_WIDTH]
        mu = _dot(ro.astype(BF16), avg_ref[...])
        dlt = ro - mu
        var = _dot((dlt * dlt).astype(BF16), avg_ref[...])
        on = dlt * lax.rsqrt(var + LN_EPS) * gn_g_ref[...] + gn_b_ref[...]
        o_ref[:, A_WIDTH:A_WIDTH + B_WIDTH] = on * (bgate * jax.nn.sigmoid(bgate))

    def group_c():
        for g in range(G):
            rows_g = slice(g * TB, (g + 1) * TB)
            kwin[g, WINDOW:WINDOW + TB, :] = z_ref[rows_g, OFF_CK:OFF_CK + C_KV_WIDTH].astype(BF16)
            vwin[g, WINDOW:WINDOW + TB, :] = z_ref[rows_g, OFF_CV:OFF_CV + C_KV_WIDTH].astype(BF16)
            for c in range(TB // QS):
                rows = slice(g * TB + c * QS, g * TB + (c + 1) * QS)
                kw = kwin[g, c * QS:c * QS + W, :]
                vw = vwin[g, c * QS:c * QS + W, :]
                for kvh in range(C_KV_HEADS):
                    qs = jnp.concatenate(
                        [z_ref[rows, OFF_CQ + (kvh * C_GROUP + j) * HEAD_DIM:OFF_CQ + (kvh * C_GROUP + j + 1) * HEAD_DIM]
                         for j in range(C_GROUP)], axis=0)
                    qs = (qs * (HEAD_DIM ** -0.5 * LOG2E)).astype(BF16)
                    s = _dot_nt(kw[:, kvh * HEAD_DIM:(kvh + 1) * HEAD_DIM], qs) - alibi_ref[kvh]
                    if (not has_past) and c == 0:
                        key_id = lax.broadcasted_iota(jnp.int32, (W, C_GROUP * QS), 0)
                        n_invalid = jnp.where(t_idx == 0, WINDOW, 0)
                        s = jnp.where(key_id >= n_invalid, s, NEG)
                    sk = sink_ref[kvh]
                    m = jnp.maximum(jnp.max(s, axis=0, keepdims=True), sk)
                    e = jnp.exp2(s - m)
                    den = jnp.sum(e, axis=0, keepdims=True) + jnp.exp2(sk - m)
                    ot = _dot_tn(vw, e.astype(BF16))
                    ot = ot[kvh * HEAD_DIM:(kvh + 1) * HEAD_DIM, :] * (1.0 / den)
                    oc = ot.T
                    oc = jnp.concatenate([oc[j * QS:(j + 1) * QS, :] for j in range(C_GROUP)], axis=1)
                    lo = A_WIDTH + B_WIDTH + kvh * C_GROUP * HEAD_DIM
                    o_ref[rows, lo:lo + C_GROUP * HEAD_DIM] = oc
            if carried:
                kwin[g, 0:WINDOW, :] = kwin[g, TB:TB + WINDOW, :]
                vwin[g, 0:WINDOW, :] = vwin[g, TB:TB + WINDOW, :]

    def new_states():
        kn = ok_ref.shape[1]
        for g in range(G):
            oconv_ref[g] = axbuf[g, SUBLANES + TB - 3:SUBLANES + TB, :]
            olru_ref[g] = hc_ref[g]
            for hd in range(B_HEADS):
                sl = slice(hd * HEAD_DIM, (hd + 1) * HEAD_DIM)
                oret_ref[g, hd] = sbd[g, sl, sl]
            last = slice((g + 1) * TB - kn, (g + 1) * TB)
            ok_ref[g] = z_ref[last, OFF_CK:OFF_CK + C_KV_WIDTH]
            ov_ref[g] = z_ref[last, OFF_CV:OFF_CV + C_KV_WIDTH]

    x = x_ref[...].reshape(R, D_MODEL)
    h = _ln(x, ln_in_g[...], ln_in_b[...]) if ln_in else x
    z_ref[...] = _dot(h.astype(BF16), w_in_ref[...])
    group_a()
    group_b()
    group_c()
    y = _dot(o_ref[...].astype(BF16), w_out_ref[...])
    h1_ref[...] = _ln(ALPHA * h + y, ln1_g[...], ln1_b[...]).reshape(G, TB, D_MODEL)
    pl.when(t_idx == NT - 1)(new_states)


def _const_spec(shape):
    nd = len(shape)
    return pl.BlockSpec(shape, lambda *_, _nd=nd: (0,) * _nd, pipeline_mode=pl.Buffered(1))


def _layer_spec(shape, layer):
    nd = len(shape) - 1
    return pl.BlockSpec((None,) + tuple(shape[1:]), lambda *_, _nd=nd: (layer,) + (0,) * _nd,
                        pipeline_mode=pl.Buffered(1))


def _mixer_call(x, states, layer, lp, tabs, *, G, TB, CR, QS, ln_in):
    B, T, _ = x.shape
    NS, NT = B // G, T // TB
    has_past = states is not None
    KN = TB if has_past else WINDOW
    cfg = (G, TB, CR, QS, NT, has_past, ln_in)

    in_specs = [pl.BlockSpec((G, TB, D_MODEL), lambda s, t: (s, t, 0))]
    args = [x]
    if has_past:
        for st in states:
            nd = st.ndim
            in_specs.append(pl.BlockSpec((None, G) + st.shape[2:],
                                         lambda s, t, _nd=nd: (layer, s) + (0,) * (_nd - 2)))
            args.append(st)
    for cst, stacked in lp:
        in_specs.append(_layer_spec(cst.shape, layer) if stacked else _const_spec(cst.shape))
        args.append(cst)
    for cst in tabs:
        in_specs.append(_const_spec(cst.shape))
        args.append(cst)

    out_shape = (
        jax.ShapeDtypeStruct((B, T, D_MODEL), F32),
        jax.ShapeDtypeStruct((B, CONV_W - 1, A_WIDTH), F32),
        jax.ShapeDtypeStruct((B, 1, A_WIDTH), F32),
        jax.ShapeDtypeStruct((B, B_HEADS, HEAD_DIM, HEAD_DIM), F32),
        jax.ShapeDtypeStruct((B, KN, C_KV_WIDTH), F32),
        jax.ShapeDtypeStruct((B, KN, C_KV_WIDTH), F32),
    )
    out_specs = (
        pl.BlockSpec((G, TB, D_MODEL), lambda s, t: (s, t, 0)),
        pl.BlockSpec((G, CONV_W - 1, A_WIDTH), lambda s, t: (s, 0, 0)),
        pl.BlockSpec((G, 1, A_WIDTH), lambda s, t: (s, 0, 0)),
        pl.BlockSpec((G, B_HEADS, HEAD_DIM, HEAD_DIM), lambda s, t: (s, 0, 0, 0)),
        pl.BlockSpec((G, KN, C_KV_WIDTH), lambda s, t: (s, 0, 0)),
        pl.BlockSpec((G, KN, C_KV_WIDTH), lambda s, t: (s, 0, 0)),
    )
    R = G * TB
    scratch = [
        pltpu.VMEM((R, D_IN), F32),
        pltpu.VMEM((R, D_MIX), F32),
        pltpu.VMEM((R, A_WIDTH), F32),
        pltpu.VMEM((G, SUBLANES + TB, A_WIDTH), F32),
        pltpu.VMEM((G, 1, A_WIDTH), F32),
        pltpu.VMEM((G, WINDOW + TB, C_KV_WIDTH), BF16),
        pltpu.VMEM((G, WINDOW + TB, C_KV_WIDTH), BF16),
        pltpu.VMEM((G, B_WIDTH, B_WIDTH), F32),
    ]
    return pl.pallas_call(
        functools.partial(_mixer_body, cfg),
        grid=(NS, NT),
        in_specs=in_specs,
        out_specs=out_specs,
        out_shape=out_shape,
        scratch_shapes=scratch,
        compiler_params=pltpu.CompilerParams(
            dimension_semantics=("arbitrary", "arbitrary"),
            vmem_limit_bytes=VMEM_LIMIT_BYTES),
        name="mixer_past" if has_past else "mixer_prompt",
    )(*args)


FFN_BOUNDS = (0, 6 * MXU_TILE, D_FF)
FFN_ROWS = 512


def _ffn_body(h_ref, wg_ref, wu_ref, wd_ref, g_ref, b_ref, out_ref):
    for r0 in range(0, h_ref.shape[0], FFN_ROWS):
        rows = slice(r0, r0 + FFN_ROWS)
        h = h_ref[rows, :]
        hb = h.astype(BF16)
        acc = jnp.zeros(h.shape, F32)
        for lo, hi in zip(FFN_BOUNDS[:-1], FFN_BOUNDS[1:]):
            cols = slice(lo, hi)
            a = _dot(hb, wg_ref[:, cols])
            b = _dot(hb, wu_ref[:, cols])
            act = (a * jax.nn.sigmoid(a) * b).astype(BF16)
            acc = acc + _dot(act, wd_ref[cols, :])
        out_ref[rows, :] = _ln(ALPHA * h + acc, g_ref[...], b_ref[...])


def _ffn_call(h2d, layer, wg, wu, wd, g, b, *, TM):
    N = h2d.shape[0]
    return pl.pallas_call(
        _ffn_body,
        grid=(N // TM,),
        in_specs=[pl.BlockSpec((TM, D_MODEL), lambda i: (i, 0))]
                 + [_layer_spec(w.shape, layer) for w in (wg, wu, wd, g, b)],
        out_specs=pl.BlockSpec((TM, D_MODEL), lambda i: (i, 0)),
        out_shape=jax.ShapeDtypeStruct((N, D_MODEL), F32),
        compiler_params=pltpu.CompilerParams(
            dimension_semantics=("arbitrary",),
            vmem_limit_bytes=VMEM_LIMIT_BYTES),
        name="ffn",
    )(h2d, wg, wu, wd, g, b)


def _block_diag(w):
    nl, hh = w.shape[0], w.shape[1]
    eye = jnp.eye(hh, dtype=w.dtype)
    return (eye[None, :, None, :, None] * w[:, :, :, None, :]).reshape(nl, hh * HEAD_DIM, hh * HEAD_DIM)


def kernel(x_prompt, x_sample, state_conv, state_lru, state_ret, cache_k, cache_v, ln_in_g, ln_in_b, w_in, conv_w, conv_b, w_rg_a, b_rg_a, w_rg_x, b_rg_x, lru_lambda, ret_gn_g, ret_gn_b, sinks, w_out, ln1_g, ln1_b, w_gate, w_up, w_down, ln2_g, ln2_b):
    nb_p, nb_s = x_prompt.shape[0], x_sample.shape[0]
    wc = cache_k.shape[2]
    rows = lambda v: v.reshape(DEPTH, 1, -1)
    w_in_b = w_in.astype(BF16)
    w_out_b = w_out.astype(BF16)
    w_gate_b = w_gate.astype(BF16)
    w_up_b = w_up.astype(BF16)
    w_down_b = w_down.astype(BF16)
    ln2_gr, ln2_br = rows(ln2_g), rows(ln2_b)

    PROMPT = dict(G=1, TB=1024, CR=128, QS=128)
    SAMPLE = dict(G=8, TB=32, CR=32, QS=32)
    FFN_TM = 1024
    tabs_p = _retention_tables(PROMPT["CR"]) + (_alibi_table(PROMPT["QS"]),)
    tabs_s = _retention_tables(SAMPLE["CR"]) + (_alibi_table(SAMPLE["QS"]),)

    w_gates = jnp.concatenate([_block_diag(w_rg_a), _block_diag(w_rg_x)], axis=2).astype(BF16)
    b_gates = jnp.concatenate([b_rg_a, b_rg_x], axis=1).reshape(DEPTH, 1, 2 * A_WIDTH)
    shared = [(w_in_b, True), (conv_w, True), (rows(conv_b), True), (w_gates, True), (b_gates, True),
              (rows(lru_lambda), True), (rows(ret_gn_g), True), (rows(ret_gn_b), True)]
    tail = [(w_out_b, True), (rows(ln1_g), True), (rows(ln1_b), True)]
    head = [(ln_in_g.reshape(1, -1), False), (ln_in_b.reshape(1, -1), False)]

    def mixer_params(qs):
        sink_rows = jnp.repeat(sinks * LOG2E, qs, axis=1).reshape(DEPTH, C_KV_HEADS, 1, C_GROUP * qs)
        return head + shared + [(sink_rows, True)] + tail

    lp_p, lp_s = mixer_params(PROMPT["QS"]), mixer_params(SAMPLE["QS"])
    states = (state_conv, state_lru.reshape(DEPTH, nb_s, 1, A_WIDTH), state_ret,
              cache_k.reshape(DEPTH, nb_s, wc, C_KV_WIDTH), cache_v.reshape(DEPTH, nb_s, wc, C_KV_WIDTH))

    hp, hs = x_prompt, x_sample
    outs_p, outs_s = [], []
    for l in range(DEPTH):
        res = _mixer_call(hp, None, l, lp_p, tabs_p, ln_in=(l == 0), **PROMPT)
        hp, st = res[0], res[1:]
        outs_p.append(st)
        hp = _ffn_call(hp.reshape(-1, D_MODEL), l, w_gate_b, w_up_b, w_down_b, ln2_gr, ln2_br,
                       TM=FFN_TM).reshape(x_prompt.shape)
        res = _mixer_call(hs, states, l, lp_s, tabs_s, ln_in=(l == 0), **SAMPLE)
        hs, st = res[0], res[1:]
        outs_s.append(st)
        hs = _ffn_call(hs.reshape(-1, D_MODEL), l, w_gate_b, w_up_b, w_down_b, ln2_gr, ln2_br,
                       TM=FFN_TM).reshape(x_sample.shape)

    def stack(outs, nb, kn):
        conv = jnp.stack([o[0] for o in outs])
        lru = jnp.stack([o[1].reshape(nb, A_WIDTH) for o in outs])
        ret = jnp.stack([o[2] for o in outs])
        kk = jnp.stack([o[3].reshape(nb, kn, C_KV_HEADS, HEAD_DIM) for o in outs])
        vv = jnp.stack([o[4].reshape(nb, kn, C_KV_HEADS, HEAD_DIM) for o in outs])
        return conv, lru, ret, kk, vv

    p_conv, p_lru, p_ret, p_k, p_v = stack(outs_p, nb_p, WINDOW)
    s_conv, s_lru, s_ret, s_k, s_v = stack(outs_s, nb_s, SAMPLE["TB"])
    return (hp, hs, p_conv, p_lru, p_ret, p_k, p_v, s_conv, s_lru, s_ret, s_k, s_v)
```

```python
import functools
import math

import numpy as np
import jax
import jax.numpy as jnp
from jax import lax
from jax.experimental import pallas as pl
from jax.experimental.pallas import tpu as pltpu

F32 = jnp.float32
BF16 = jnp.bfloat16

D_MODEL = 1024
DEPTH = 4
HEAD_DIM = 64
A_WIDTH = 256
B_HEADS = 4
B_WIDTH = 256
C_HEADS = 8
C_KV_HEADS = 2
C_GROUP = 4
C_WIDTH = 512
C_KV_WIDTH = 128
CONV_W = 4
LRU_C = 8.0
WINDOW = 128
ATTN_CHUNK = 64
D_MIX = 1024
D_IN = 2304
D_FF = 2816
ALPHA = (2.0 * DEPTH) ** 0.25
LN_EPS = 1e-5
NEG = -1e30
LOG2E = math.log2(math.e)

OFF_AX, OFF_AG = 0, 256
OFF_BQ, OFF_BK, OFF_BV, OFF_BG = 512, 768, 1024, 1280
OFF_CQ, OFF_CK, OFF_CV = 1536, 2048, 2176

SUBLANES = 8
MXU_TILE = 256
VMEM_LIMIT_BYTES = 52 * 1024 * 1024


def _ln(x, g, b):
    mu = jnp.mean(x, axis=-1, keepdims=True)
    d = x - mu
    var = jnp.mean(d * d, axis=-1, keepdims=True)
    return d * lax.rsqrt(var + LN_EPS) * g + b


def _dot(a, b):
    return jnp.dot(a, b, preferred_element_type=F32)


def _dot_nt(a, b):
    return lax.dot_general(a, b, (((1,), (1,)), ((), ())), preferred_element_type=F32)


def _dot_tn(a, b):
    return lax.dot_general(a, b, (((0,), (0,)), ((), ())), preferred_element_type=F32)


def _retention_tables(chunk):
    lg = np.log1p(-(2.0 ** (-5.0 - np.arange(B_HEADS, dtype=np.float64))))
    idx = np.arange(chunk, dtype=np.float64)
    diff = idx[:, None] - idx[None, :]
    dmat = np.where(diff >= 0, np.exp(np.maximum(diff, 0.0)[None] * lg[:, None, None]), 0.0)
    dmat = dmat * HEAD_DIM ** -0.5
    dmat_all = np.concatenate([dmat[h] for h in range(B_HEADS)], axis=1)
    w_start = np.exp((idx + 1.0)[:, None] * lg[None, :])
    w_end = np.exp((chunk - 1.0 - idx)[:, None] * lg[None, :]) * HEAD_DIM ** -0.5
    w_start = np.repeat(w_start, HEAD_DIM, axis=1)
    w_end = np.repeat(w_end, HEAD_DIM, axis=1)
    g_chunk = np.repeat(np.exp(chunk * lg), HEAD_DIM)
    head = np.arange(B_WIDTH) // HEAD_DIM
    blk = (head[:, None] == head[None, :]).astype(np.float64)
    g_bd = blk * g_chunk[:, None]
    avg_bd = blk / HEAD_DIM
    return (jnp.asarray(dmat_all, F32), jnp.asarray(w_start, F32), jnp.asarray(w_end, F32),
            jnp.asarray(g_bd, F32), jnp.asarray(blk, F32), jnp.asarray(avg_bd, BF16))


def _alibi_table(qs):
    w = WINDOW + qs
    slopes = 2.0 ** (-8.0 * np.arange(1, C_HEADS + 1, dtype=np.float64) / C_HEADS)
    qpos = np.arange(qs)
    kpos = np.arange(w) - WINDOW
    dist = np.abs(qpos[:, None] - kpos[None, :]).astype(np.float64)
    q_chunk = qpos // ATTN_CHUNK
    k_chunk = np.floor_divide(kpos, ATTN_CHUNK)
    visible = (k_chunk[None, :] <= q_chunk[:, None]) & (k_chunk[None, :] >= q_chunk[:, None] - WINDOW // ATTN_CHUNK)
    tab = slopes[:, None, None] * dist[None]
    tab = np.where(visible[None], tab * LOG2E, -NEG)
    tab = tab.reshape(C_KV_HEADS, C_GROUP * qs, w).transpose(0, 2, 1)
    return jnp.asarray(tab, F32)


def _mixer_body(cfg, *refs):
    G, TB, CR, QS, NT, has_past, ln_in = cfg
    R = G * TB
    W = WINDOW + QS
    carried = NT > 1
    it = iter(refs)
    x_ref = next(it)
    if has_past:
        sconv_ref, slru_ref, sret_ref, ck_ref, cv_ref = (next(it) for _ in range(5))
    (ln_in_g, ln_in_b, w_in_ref, conv_w_ref, conv_b_ref, w_gate_ref, b_gate_ref, lam_ref,
     gn_g_ref, gn_b_ref, sink_ref, w_out_ref, ln1_g, ln1_b,
     dmat_ref, wstart_ref, wend_ref, gbd_ref, blk_ref, avg_ref, alibi_ref) = (next(it) for _ in range(21))
    h1_ref, oconv_ref, olru_ref, oret_ref, ok_ref, ov_ref = (next(it) for _ in range(6))
    z_ref, o_ref, sb_ref, axbuf, hc_ref, kwin, vwin, sbd = (next(it) for _ in range(8))

    t_idx = pl.program_id(1)

    @pl.when(t_idx == 0)
    def _init():
        for g in range(G):
            if has_past:
                axbuf[g, SUBLANES - 3:SUBLANES, :] = sconv_ref[g]
                hc_ref[g] = slru_ref[g]
                sbd[g] = jnp.zeros((B_WIDTH, B_WIDTH), F32)
                for hd in range(B_HEADS):
                    sl = slice(hd * HEAD_DIM, (hd + 1) * HEAD_DIM)
                    sbd[g, sl, sl] = sret_ref[g, hd]
                kwin[g, 0:WINDOW, :] = ck_ref[g].astype(BF16)
                vwin[g, 0:WINDOW, :] = cv_ref[g].astype(BF16)
            else:
                axbuf[g, 0:SUBLANES, :] = jnp.zeros((SUBLANES, A_WIDTH), F32)
                hc_ref[g] = jnp.zeros((1, A_WIDTH), F32)
                sbd[g] = jnp.zeros((B_WIDTH, B_WIDTH), F32)
                kwin[g, 0:WINDOW, :] = jnp.zeros((WINDOW, C_KV_WIDTH), BF16)
                vwin[g, 0:WINDOW, :] = jnp.zeros((WINDOW, C_KV_WIDTH), BF16)

    def group_a():
        nl = -lam_ref[...]
        softplus_nl = jnp.maximum(nl, 0.0) + jnp.log1p(jnp.exp(-jnp.abs(nl)))
        cw = conv_w_ref[...]
        row_in_tile = lax.broadcasted_iota(jnp.int32, (TB // SUBLANES, SUBLANES, A_WIDTH), 1)
        for g in range(G):
            rows = slice(g * TB, (g + 1) * TB)
            ax = z_ref[rows, OFF_AX:OFF_AX + A_WIDTH]
            axbuf[g, SUBLANES:SUBLANES + TB, :] = ax
            u = conv_b_ref[...] + axbuf[g, SUBLANES - 3:SUBLANES - 3 + TB, :] * cw[0:1, :]
            u = u + axbuf[g, SUBLANES - 2:SUBLANES - 2 + TB, :] * cw[1:2, :]
            u = u + axbuf[g, SUBLANES - 1:SUBLANES - 1 + TB, :] * cw[2:3, :]
            u = u + ax * cw[3:4, :]
            if carried:
                axbuf[g, SUBLANES - 3:SUBLANES, :] = axbuf[g, SUBLANES + TB - 3:SUBLANES + TB, :]
            gz = _dot(u.astype(BF16), w_gate_ref[...]) + b_gate_ref[...]
            r = jax.nn.sigmoid(gz[:, 0:A_WIDTH])
            gi = jax.nn.sigmoid(gz[:, A_WIDTH:2 * A_WIDTH])
            log_a = (-LRU_C) * r * softplus_nl
            a = jnp.exp(log_a)
            th = jnp.tanh(log_a)
            one_minus_a2 = (-2.0) * th / (1.0 - th)
            bt = jnp.sqrt(one_minus_a2) * (gi * u)
            a = a.reshape(TB // SUBLANES, SUBLANES, A_WIDTH)
            bt = bt.reshape(TB // SUBLANES, SUBLANES, A_WIDTH)
            for s in (1, 2, 4):
                keep = row_in_tile >= s
                bt = jnp.where(keep, a * pltpu.roll(bt, s, 1) + bt, bt)
                a = jnp.where(keep, a * pltpu.roll(a, s, 1), a)
            hprev = jnp.broadcast_to(hc_ref[g], (SUBLANES, A_WIDTH))
            for i in range(TB // SUBLANES):
                ht = a[i] * hprev + bt[i]
                sb_ref[g * TB + i * SUBLANES:g * TB + (i + 1) * SUBLANES, :] = ht
                hprev = jnp.broadcast_to(ht[SUBLANES - 1:SUBLANES, :], (SUBLANES, A_WIDTH))
            hc_ref[g] = hprev[0:1, :]
            ha = sb_ref[rows, :]
            ag = z_ref[rows, OFF_AG:OFF_AG + A_WIDTH]
            o_ref[rows, 0:A_WIDTH] = ha * jax.nn.gelu(ag)

    def group_b():
        lane_head = lax.broadcasted_iota(jnp.int32, (CR, B_WIDTH), 1) // HEAD_DIM
        for g in range(G):
            for c in range(TB // CR):
                rows = slice(g * TB + c * CR, g * TB + (c + 1) * CR)
                q = z_ref[rows, OFF_BQ:OFF_BQ + B_WIDTH]
                k = z_ref[rows, OFF_BK:OFF_BK + B_WIDTH]
                v = z_ref[rows, OFF_BV:OFF_BV + B_WIDTH]
                kx = jnp.concatenate([jnp.where(lane_head == hh, k, 0.0) for hh in range(B_HEADS)], axis=0).astype(BF16)
                vx = jnp.concatenate([jnp.where(lane_head == hh, v, 0.0) for hh in range(B_HEADS)], axis=0).astype(BF16)
                inner = _dot_nt(q.astype(BF16), kx) * dmat_ref[...]
                s_prev = sbd[g]
                ro = _dot(inner.astype(BF16), vx) + _dot((q * wstart_ref[...]).astype(BF16), s_prev.astype(BF16))
                upd = _dot_tn((k * wend_ref[...]).astype(BF16), v.astype(BF16))
                sbd[g] = gbd_ref[...] * s_prev + upd * blk_ref[...]
                o_ref[rows, A_WIDTH:A_WIDTH + B_WIDTH] = ro
        ro = o_ref[:, A_WIDTH:A_WIDTH + B_WIDTH]
        bgate = z_ref[:, OFF_BG:OFF_BG + B_WIDTH]
        mu = _dot(ro.astype(BF16), avg_ref[...])
        dlt = ro - mu
        var = _dot((dlt * dlt).astype(BF16), avg_ref[...])
        on = dlt * lax.rsqrt(var + LN_EPS) * gn_g_ref[...] + gn_b_ref[...]
        o_ref[:, A_WIDTH:A_WIDTH + B_WIDTH] = on * (bgate * jax.nn.sigmoid(bgate))

    def group_c():
        for g in range(G):
            rows_g = slice(g * TB, (g + 1) * TB)
            kwin[g, WINDOW:WINDOW + TB, :] = z_ref[rows_g, OFF_CK:OFF_CK + C_KV_WIDTH].astype(BF16)
            vwin[g, WINDOW:WINDOW + TB, :] = z_ref[rows_g, OFF_CV:OFF_CV + C_KV_WIDTH].astype(BF16)
            for c in range(TB // QS):
                rows = slice(g * TB + c * QS, g * TB + (c + 1) * QS)
                kw = kwin[g, c * QS:c * QS + W, :]
                vw = vwin[g, c * QS:c * QS + W, :]
                for kvh in range(C_KV_HEADS):
                    qs = jnp.concatenate(
                        [z_ref[rows, OFF_CQ + (kvh * C_GROUP + j) * HEAD_DIM:OFF_CQ + (kvh * C_GROUP + j + 1) * HEAD_DIM]
                         for j in range(C_GROUP)], axis=0)
                    qs = (qs * (HEAD_DIM ** -0.5 * LOG2E)).astype(BF16)
                    s = _dot_nt(kw[:, kvh * HEAD_DIM:(kvh + 1) * HEAD_DIM], qs) - alibi_ref[kvh]
                    if (not has_past) and c == 0:
                        key_id = lax.broadcasted_iota(jnp.int32, (W, C_GROUP * QS), 0)
                        n_invalid = jnp.where(t_idx == 0, WINDOW, 0)
                        s = jnp.where(key_id >= n_invalid, s, NEG)
                    sk = sink_ref[kvh]
                    m = jnp.maximum(jnp.max(s, axis=0, keepdims=True), sk)
                    e = jnp.exp2(s - m)
                    den = jnp.sum(e, axis=0, keepdims=True) + jnp.exp2(sk - m)
                    ot = _dot_tn(vw, e.astype(BF16))
                    ot = ot[kvh * HEAD_DIM:(kvh + 1) * HEAD_DIM, :] * (1.0 / den)
                    oc = ot.T
                    oc = jnp.concatenate([oc[j * QS:(j + 1) * QS, :] for j in range(C_GROUP)], axis=1)
                    lo = A_WIDTH + B_WIDTH + kvh * C_GROUP * HEAD_DIM
                    o_ref[rows, lo:lo + C_GROUP * HEAD_DIM] = oc
            if carried:
                kwin[g, 0:WINDOW, :] = kwin[g, TB:TB + WINDOW, :]
                vwin[g, 0:WINDOW, :] = vwin[g, TB:TB + WINDOW, :]

    def new_states():
        kn = ok_ref.shape[1]
        for g in range(G):
            oconv_ref[g] = axbuf[g, SUBLANES + TB - 3:SUBLANES + TB, :]
            olru_ref[g] = hc_ref[g]
            for hd in range(B_HEADS):
                sl = slice(hd * HEAD_DIM, (hd + 1) * HEAD_DIM)
                oret_ref[g, hd] = sbd[g, sl, sl]
            last = slice((g + 1) * TB - kn, (g + 1) * TB)
            ok_ref[g] = z_ref[last, OFF_CK:OFF_CK + C_KV_WIDTH]
            ov_ref[g] = z_ref[last, OFF_CV:OFF_CV + C_KV_WIDTH]

    x = x_ref[...].reshape(R, D_MODEL)
    h = _ln(x, ln_in_g[...], ln_in_b[...]) if ln_in else x
    z_ref[...] = _dot(h.astype(BF16), w_in_ref[...])
    group_a()
    group_b()
    group_c()
    y = _dot(o_ref[...].astype(BF16), w_out_ref[...])
    h1_ref[...] = _ln(ALPHA * h + y, ln1_g[...], ln1_b[...]).reshape(G, TB, D_MODEL)
    pl.when(t_idx == NT - 1)(new_states)


def _const_spec(shape):
    nd = len(shape)
    return pl.BlockSpec(shape, lambda *_, _nd=nd: (0,) * _nd, pipeline_mode=pl.Buffered(1))


def _layer_spec(shape, layer):
    nd = len(shape) - 1
    return pl.BlockSpec((None,) + tuple(shape[1:]), lambda *_, _nd=nd: (layer,) + (0,) * _nd,
                        pipeline_mode=pl.Buffered(1))


def _mixer_call(x, states, layer, lp, tabs, *, G, TB, CR, QS, ln_in):
    B, T, _ = x.shape
    NS, NT = B // G, T // TB
    has_past = states is not None
    KN = TB if has_past else WINDOW
    cfg = (G, TB, CR, QS, NT, has_past, ln_in)

    in_specs = [pl.BlockSpec((G, TB, D_MODEL), lambda s, t: (s, t, 0))]
    args = [x]
    if has_past:
        for st in states:
            nd = st.ndim
            in_specs.append(pl.BlockSpec((None, G) + st.shape[2:],
                                         lambda s, t, _nd=nd: (layer, s) + (0,) * (_nd - 2)))
            args.append(st)
    for cst, stacked in lp:
        in_specs.append(_layer_spec(cst.shape, layer) if stacked else _const_spec(cst.shape))
        args.append(cst)
    for cst in tabs:
        in_specs.append(_const_spec(cst.shape))
        args.append(cst)

    out_shape = (
        jax.ShapeDtypeStruct((B, T, D_MODEL), F32),
        jax.ShapeDtypeStruct((B, CONV_W - 1, A_WIDTH), F32),
        jax.ShapeDtypeStruct((B, 1, A_WIDTH), F32),
        jax.ShapeDtypeStruct((B, B_HEADS, HEAD_DIM, HEAD_DIM), F32),
        jax.ShapeDtypeStruct((B, KN, C_KV_WIDTH), F32),
        jax.ShapeDtypeStruct((B, KN, C_KV_WIDTH), F32),
    )
    out_specs = (
        pl.BlockSpec((G, TB, D_MODEL), lambda s, t: (s, t, 0)),
        pl.BlockSpec((G, CONV_W - 1, A_WIDTH), lambda s, t: (s, 0, 0)),
        pl.BlockSpec((G, 1, A_WIDTH), lambda s, t: (s, 0, 0)),
        pl.BlockSpec((G, B_HEADS, HEAD_DIM, HEAD_DIM), lambda s, t: (s, 0, 0, 0)),
        pl.BlockSpec((G, KN, C_KV_WIDTH), lambda s, t: (s, 0, 0)),
        pl.BlockSpec((G, KN, C_KV_WIDTH), lambda s, t: (s, 0, 0)),
    )
    R = G * TB
    scratch = [
        pltpu.VMEM((R, D_IN), F32),
        pltpu.VMEM((R, D_MIX), F32),
        pltpu.VMEM((R, A_WIDTH), F32),
        pltpu.VMEM((G, SUBLANES + TB, A_WIDTH), F32),
        pltpu.VMEM((G, 1, A_WIDTH), F32),
        pltpu.VMEM((G, WINDOW + TB, C_KV_WIDTH), BF16),
        pltpu.VMEM((G, WINDOW + TB, C_KV_WIDTH), BF16),
        pltpu.VMEM((G, B_WIDTH, B_WIDTH), F32),
    ]
    return pl.pallas_call(
        functools.partial(_mixer_body, cfg),
        grid=(NS, NT),
        in_specs=in_specs,
        out_specs=out_specs,
        out_shape=out_shape,
        scratch_shapes=scratch,
        compiler_params=pltpu.CompilerParams(
            dimension_semantics=("arbitrary", "arbitrary"),
            vmem_limit_bytes=VMEM_LIMIT_BYTES),
        name="mixer_past" if has_past else "mixer_prompt",
    )(*args)


FFN_BOUNDS = (0, 6 * MXU_TILE, D_FF)
FFN_ROWS = 512


def _ffn_body(h_ref, wg_ref, wu_ref, wd_ref, g_ref, b_ref, out_ref):
    for r0 in range(0, h_ref.shape[0], FFN_ROWS):
        rows = slice(r0, r0 + FFN_ROWS)
        h = h_ref[rows, :]
        hb = h.astype(BF16)
        acc = jnp.zeros(h.shape, F32)
        for lo, hi in zip(FFN_BOUNDS[:-1], FFN_BOUNDS[1:]):
            cols = slice(lo, hi)
            a = _dot(hb, wg_ref[:, cols])
            b = _dot(hb, wu_ref[:, cols])
            act = (a * jax.nn.sigmoid(a) * b).astype(BF16)
            acc = acc + _dot(act, wd_ref[cols, :])
        out_ref[rows, :] = _ln(ALPHA * h + acc, g_ref[...], b_ref[...])


def _ffn_call(h2d, layer, wg, wu, wd, g, b, *, TM):
    N = h2d.shape[0]
    return pl.pallas_call(
        _ffn_body,
        grid=(N // TM,),
        in_specs=[pl.BlockSpec((TM, D_MODEL), lambda i: (i, 0))]
                 + [_layer_spec(w.shape, layer) for w in (wg, wu, wd, g, b)],
        out_specs=pl.BlockSpec((TM, D_MODEL), lambda i: (i, 0)),
        out_shape=jax.ShapeDtypeStruct((N, D_MODEL), F32),
        compiler_params=pltpu.CompilerParams(
            dimension_semantics=("arbitrary",),
            vmem_limit_bytes=VMEM_LIMIT_BYTES),
        name="ffn",
    )(h2d, wg, wu, wd, g, b)


def _block_diag(w):
    nl, hh = w.shape[0], w.shape[1]
    eye = jnp.eye(hh, dtype=w.dtype)
    return (eye[None, :, None, :, None] * w[:, :, :, None, :]).reshape(nl, hh * HEAD_DIM, hh * HEAD_DIM)


def kernel(x_prompt, x_sample, state_conv, state_lru, state_ret, cache_k, cache_v, ln_in_g, ln_in_b, w_in, conv_w, conv_b, w_rg_a, b_rg_a, w_rg_x, b_rg_x, lru_lambda, ret_gn_g, ret_gn_b, sinks, w_out, ln1_g, ln1_b, w_gate, w_up, w_down, ln2_g, ln2_b):
    nb_p, nb_s = x_prompt.shape[0], x_sample.shape[0]
    wc = cache_k.shape[2]
    rows = lambda v: v.reshape(DEPTH, 1, -1)
    w_in_b = w_in.astype(BF16)
    w_out_b = w_out.astype(BF16)
    w_gate_b = w_gate.astype(BF16)
    w_up_b = w_up.astype(BF16)
    w_down_b = w_down.astype(BF16)
    ln2_gr, ln2_br = rows(ln2_g), rows(ln2_b)

    PROMPT = dict(G=2, TB=512, CR=256, QS=128)
    SAMPLE = dict(G=8, TB=32, CR=32, QS=32)
    FFN_TM = 1024
    tabs_p = _retention_tables(PROMPT["CR"]) + (_alibi_table(PROMPT["QS"]),)
    tabs_s = _retention_tables(SAMPLE["CR"]) + (_alibi_table(SAMPLE["QS"]),)

    w_gates = jnp.concatenate([_block_diag(w_rg_a), _block_diag(w_rg_x)], axis=2).astype(BF16)
    b_gates = jnp.concatenate([b_rg_a, b_rg_x], axis=1).reshape(DEPTH, 1, 2 * A_WIDTH)
    shared = [(w_in_b, True), (conv_w, True), (rows(conv_b), True), (w_gates, True), (b_gates, True),
              (rows(lru_lambda), True), (rows(ret_gn_g), True), (rows(ret_gn_b), True)]
    tail = [(w_out_b, True), (rows(ln1_g), True), (rows(ln1_b), True)]
    head = [(ln_in_g.reshape(1, -1), False), (ln_in_b.reshape(1, -1), False)]

    def mixer_params(qs):
        sink_rows = jnp.repeat(sinks * LOG2E, qs, axis=1).reshape(DEPTH, C_KV_HEADS, 1, C_GROUP * qs)
        return head + shared + [(sink_rows, True)] + tail

    lp_p, lp_s = mixer_params(PROMPT["QS"]), mixer_params(SAMPLE["QS"])
    states = (state_conv, state_lru.reshape(DEPTH, nb_s, 1, A_WIDTH), state_ret,
              cache_k.reshape(DEPTH, nb_s, wc, C_KV_WIDTH), cache_v.reshape(DEPTH, nb_s, wc, C_KV_WIDTH))

    hp, hs = x_prompt, x_sample
    outs_p, outs_s = [], []
    for l in range(DEPTH):
        res = _mixer_call(hp, None, l, lp_p, tabs_p, ln_in=(l == 0), **PROMPT)
        hp, st = res[0], res[1:]
        outs_p.append(st)
        hp = _ffn_call(hp.reshape(-1, D_MODEL), l, w_gate_b, w_up_b, w_down_b, ln2_gr, ln2_br,
                       TM=FFN_TM).reshape(x_prompt.shape)
        res = _mixer_call(hs, states, l, lp_s, tabs_s, ln_in=(l == 0), **SAMPLE)
        hs, st = res[0], res[1:]
        outs_s.append(st)
        hs = _ffn_call(hs.reshape(-1, D_MODEL), l, w_gate_b, w_up_b, w_down_b, ln2_gr, ln2_br,
                       TM=FFN_TM).reshape(x_sample.shape)

    def stack(outs, nb, kn):
        conv = jnp.stack([o[0] for o in outs])
        lru = jnp.stack([o[1].reshape(nb, A_WIDTH) for o in outs])
        ret = jnp.stack([o[2] for o in outs])
        kk = jnp.stack([o[3].reshape(nb, kn, C_KV_HEADS, HEAD_DIM) for o in outs])
        vv = jnp.stack([o[4].reshape(nb, kn, C_KV_HEADS, HEAD_DIM) for o in outs])
        return conv, lru, ret, kk, vv

    p_conv, p_lru, p_ret, p_k, p_v = stack(outs_p, nb_p, WINDOW)
    s_conv, s_lru, s_ret, s_k, s_v = stack(outs_s, nb_s, SAMPLE["TB"])
    return (hp, hs, p_conv, p_lru, p_ret, p_k, p_v, s_conv, s_lru, s_ret, s_k, s_v)
```

```python
import functools
import math

import numpy as np
import jax
import jax.numpy as jnp
from jax import lax
from jax.experimental import pallas as pl
from jax.experimental.pallas import tpu as pltpu

F32 = jnp.float32
BF16 = jnp.bfloat16

D_MODEL = 1024
DEPTH = 4
HEAD_DIM = 64
A_WIDTH = 256
B_HEADS = 4
B_WIDTH = 256
C_HEADS = 8
C_KV_HEADS = 2
C_GROUP = 4
C_WIDTH = 512
C_KV_WIDTH = 128
CONV_W = 4
LRU_C = 8.0
WINDOW = 128
ATTN_CHUNK = 64
D_MIX = 1024
D_IN = 2304
D_FF = 2816
ALPHA = (2.0 * DEPTH) ** 0.25
LN_EPS = 1e-5
NEG = -1e30
LOG2E = math.log2(math.e)

OFF_AX, OFF_AG = 0, 256
OFF_BQ, OFF_BK, OFF_BV, OFF_BG = 512, 768, 1024, 1280
OFF_CQ, OFF_CK, OFF_CV = 1536, 2048, 2176

SUBLANES = 8
MXU_TILE = 256
VMEM_LIMIT_BYTES = 52 * 1024 * 1024


def _ln(x, g, b):
    mu = jnp.mean(x, axis=-1, keepdims=True)
    d = x - mu
    var = jnp.mean(d * d, axis=-1, keepdims=True)
    return d * lax.rsqrt(var + LN_EPS) * g + b


def _dot(a, b):
    return jnp.dot(a, b, preferred_element_type=F32)


def _dot_nt(a, b):
    return lax.dot_general(a, b, (((1,), (1,)), ((), ())), preferred_element_type=F32)


def _dot_tn(a, b):
    return lax.dot_general(a, b, (((0,), (0,)), ((), ())), preferred_element_type=F32)


def _retention_tables(chunk):
    lg = np.log1p(-(2.0 ** (-5.0 - np.arange(B_HEADS, dtype=np.float64))))
    idx = np.arange(chunk, dtype=np.float64)
    diff = idx[:, None] - idx[None, :]
    dmat = np.where(diff >= 0, np.exp(np.maximum(diff, 0.0)[None] * lg[:, None, None]), 0.0)
    dmat = dmat * HEAD_DIM ** -0.5
    dmat_all = np.concatenate([dmat[h] for h in range(B_HEADS)], axis=1)
    w_start = np.exp((idx + 1.0)[:, None] * lg[None, :])
    w_end = np.exp((chunk - 1.0 - idx)[:, None] * lg[None, :]) * HEAD_DIM ** -0.5
    w_start = np.repeat(w_start, HEAD_DIM, axis=1)
    w_end = np.repeat(w_end, HEAD_DIM, axis=1)
    g_chunk = np.repeat(np.exp(chunk * lg), HEAD_DIM)
    head = np.arange(B_WIDTH) // HEAD_DIM
    blk = (head[:, None] == head[None, :]).astype(np.float64)
    g_bd = blk * g_chunk[:, None]
    avg_bd = blk / HEAD_DIM
    return (jnp.asarray(dmat_all, F32), jnp.asarray(w_start, F32), jnp.asarray(w_end, F32),
            jnp.asarray(g_bd, F32), jnp.asarray(blk, F32), jnp.asarray(avg_bd, BF16))


def _alibi_table(qs):
    w = WINDOW + qs
    slopes = 2.0 ** (-8.0 * np.arange(1, C_HEADS + 1, dtype=np.float64) / C_HEADS)
    qpos = np.arange(qs)
    kpos = np.arange(w) - WINDOW
    dist = np.abs(qpos[:, None] - kpos[None, :]).astype(np.float64)
    q_chunk = qpos // ATTN_CHUNK
    k_chunk = np.floor_divide(kpos, ATTN_CHUNK)
    visible = (k_chunk[None, :] <= q_chunk[:, None]) & (k_chunk[None, :] >= q_chunk[:, None] - WINDOW // ATTN_CHUNK)
    tab = slopes[:, None, None] * dist[None]
    tab = np.where(visible[None], tab * LOG2E, -NEG)
    tab = tab.reshape(C_KV_HEADS, C_GROUP * qs, w).transpose(0, 2, 1)
    return jnp.asarray(tab, F32)


def _mixer_body(cfg, *refs):
    G, TB, CR, QS, NT, has_past, ln_in = cfg
    R = G * TB
    W = WINDOW + QS
    carried = NT > 1
    it = iter(refs)
    x_ref = next(it)
    if has_past:
        sconv_ref, slru_ref, sret_ref, ck_ref, cv_ref = (next(it) for _ in range(5))
    (ln_in_g, ln_in_b, w_in_ref, conv_w_ref, conv_b_ref, w_gate_ref, b_gate_ref, lam_ref,
     gn_g_ref, gn_b_ref, sink_ref, w_out_ref, ln1_g, ln1_b,
     dmat_ref, wstart_ref, wend_ref, gbd_ref, blk_ref, avg_ref, alibi_ref) = (next(it) for _ in range(21))
    h1_ref, oconv_ref, olru_ref, oret_ref, ok_ref, ov_ref = (next(it) for _ in range(6))
    z_ref, o_ref, sb_ref, axbuf, hc_ref, kwin, vwin, sbd = (next(it) for _ in range(8))

    t_idx = pl.program_id(1)

    @pl.when(t_idx == 0)
    def _init():
        for g in range(G):
            if has_past:
                axbuf[g, SUBLANES - 3:SUBLANES, :] = sconv_ref[g]
                hc_ref[g] = slru_ref[g]
                sbd[g] = jnp.zeros((B_WIDTH, B_WIDTH), F32)
                for hd in range(B_HEADS):
                    sl = slice(hd * HEAD_DIM, (hd + 1) * HEAD_DIM)
                    sbd[g, sl, sl] = sret_ref[g, hd]
                kwin[g, 0:WINDOW, :] = ck_ref[g].astype(BF16)
                vwin[g, 0:WINDOW, :] = cv_ref[g].astype(BF16)
            else:
                axbuf[g, 0:SUBLANES, :] = jnp.zeros((SUBLANES, A_WIDTH), F32)
                hc_ref[g] = jnp.zeros((1, A_WIDTH), F32)
                sbd[g] = jnp.zeros((B_WIDTH, B_WIDTH), F32)
                kwin[g, 0:WINDOW, :] = jnp.zeros((WINDOW, C_KV_WIDTH), BF16)
                vwin[g, 0:WINDOW, :] = jnp.zeros((WINDOW, C_KV_WIDTH), BF16)

    def group_a():
        nl = -lam_ref[...]
        softplus_nl = jnp.maximum(nl, 0.0) + jnp.log1p(jnp.exp(-jnp.abs(nl)))
        cw = conv_w_ref[...]
        row_in_tile = lax.broadcasted_iota(jnp.int32, (TB // SUBLANES, SUBLANES, A_WIDTH), 1)
        for g in range(G):
            rows = slice(g * TB, (g + 1) * TB)
            ax = z_ref[rows, OFF_AX:OFF_AX + A_WIDTH]
            axbuf[g, SUBLANES:SUBLANES + TB, :] = ax
            u = conv_b_ref[...] + axbuf[g, SUBLANES - 3:SUBLANES - 3 + TB, :] * cw[0:1, :]
            u = u + axbuf[g, SUBLANES - 2:SUBLANES - 2 + TB, :] * cw[1:2, :]
            u = u + axbuf[g, SUBLANES - 1:SUBLANES - 1 + TB, :] * cw[2:3, :]
            u = u + ax * cw[3:4, :]
            if carried:
                axbuf[g, SUBLANES - 3:SUBLANES, :] = axbuf[g, SUBLANES + TB - 3:SUBLANES + TB, :]
            gz = _dot(u.astype(BF16), w_gate_ref[...]) + b_gate_ref[...]
            r = jax.nn.sigmoid(gz[:, 0:A_WIDTH])
            gi = jax.nn.sigmoid(gz[:, A_WIDTH:2 * A_WIDTH])
            log_a = (-LRU_C) * r * softplus_nl
            a = jnp.exp(log_a)
            th = jnp.tanh(log_a)
            one_minus_a2 = (-2.0) * th / (1.0 - th)
            bt = jnp.sqrt(one_minus_a2) * (gi * u)
            a = a.reshape(TB // SUBLANES, SUBLANES, A_WIDTH)
            bt = bt.reshape(TB // SUBLANES, SUBLANES, A_WIDTH)
            for s in (1, 2, 4):
                keep = row_in_tile >= s
                bt = jnp.where(keep, a * pltpu.roll(bt, s, 1) + bt, bt)
                a = jnp.where(keep, a * pltpu.roll(a, s, 1), a)
            hprev = jnp.broadcast_to(hc_ref[g], (SUBLANES, A_WIDTH))
            for i in range(TB // SUBLANES):
                ht = a[i] * hprev + bt[i]
                sb_ref[g * TB + i * SUBLANES:g * TB + (i + 1) * SUBLANES, :] = ht
                hprev = jnp.broadcast_to(ht[SUBLANES - 1:SUBLANES, :], (SUBLANES, A_WIDTH))
            hc_ref[g] = hprev[0:1, :]
            ha = sb_ref[rows, :]
            ag = z_ref[rows, OFF_AG:OFF_AG + A_WIDTH]
            o_ref[rows, 0:A_WIDTH] = ha * jax.nn.gelu(ag)

    def group_b():
        lane_head = lax.broadcasted_iota(jnp.int32, (CR, B_WIDTH), 1) // HEAD_DIM
        for g in range(G):
            for c in range(TB // CR):
                rows = slice(g * TB + c * CR, g * TB + (c + 1) * CR)
                q = z_ref[rows, OFF_BQ:OFF_BQ + B_WIDTH]
                k = z_ref[rows, OFF_BK:OFF_BK + B_WIDTH]
                v = z_ref[rows, OFF_BV:OFF_BV + B_WIDTH]
                kx = jnp.concatenate([jnp.where(lane_head == hh, k, 0.0) for hh in range(B_HEADS)], axis=0).astype(BF16)
                vx = jnp.concatenate([jnp.where(lane_head == hh, v, 0.0) for hh in range(B_HEADS)], axis=0).astype(BF16)
                inner = _dot_nt(q.astype(BF16), kx) * dmat_ref[...]
                s_prev = sbd[g]
                ro = _dot(inner.astype(BF16), vx) + _dot((q * wstart_ref[...]).astype(BF16), s_prev.astype(BF16))
                upd = _dot_tn((k * wend_ref[...]).astype(BF16), v.astype(BF16))
                sbd[g] = gbd_ref[...] * s_prev + upd * blk_ref[...]
                o_ref[rows, A_WIDTH:A_WIDTH + B_WIDTH] = ro
        ro = o_ref[:, A_WIDTH:A_WIDTH + B_WIDTH]
        bgate = z_ref[:, OFF_BG:OFF_BG + B_WIDTH]
        mu = _dot(ro.astype(BF16), avg_ref[...])
        dlt = ro - mu
        var = _dot((dlt * dlt).astype(BF16), avg_ref[...])
        on = dlt * lax.rsqrt(var + LN_EPS) * gn_g_ref[...] + gn_b_ref[...]
        o_ref[:, A_WIDTH:A_WIDTH + B_WIDTH] = on * (bgate * jax.nn.sigmoid(bgate))

    def group_c():
        for g in range(G):
            rows_g = slice(g * TB, (g + 1) * TB)
            kwin[g, WINDOW:WINDOW + TB, :] = z_ref[rows_g, OFF_CK:OFF_CK + C_KV_WIDTH].astype(BF16)
            vwin[g, WINDOW:WINDOW + TB, :] = z_ref[rows_g, OFF_CV:OFF_CV + C_KV_WIDTH].astype(BF16)
            for c in range(TB // QS):
                rows = slice(g * TB + c * QS, g * TB + (c + 1) * QS)
                kw = kwin[g, c * QS:c * QS + W, :]
                vw = vwin[g, c * QS:c * QS + W, :]
                for kvh in range(C_KV_HEADS):
                    qs = jnp.concatenate(
                        [z_ref[rows, OFF_CQ + (kvh * C_GROUP + j) * HEAD_DIM:OFF_CQ + (kvh * C_GROUP + j + 1) * HEAD_DIM]
                         for j in range(C_GROUP)], axis=0)
                    qs = (qs * (HEAD_DIM ** -0.5 * LOG2E)).astype(BF16)
                    s = _dot_nt(kw[:, kvh * HEAD_DIM:(kvh + 1) * HEAD_DIM], qs) - alibi_ref[kvh]
                    if (not has_past) and c == 0:
                        key_id = lax.broadcasted_iota(jnp.int32, (W, C_GROUP * QS), 0)
                        n_invalid = jnp.where(t_idx == 0, WINDOW, 0)
                        s = jnp.where(key_id >= n_invalid, s, NEG)
                    sk = sink_ref[kvh]
                    m = jnp.maximum(jnp.max(s, axis=0, keepdims=True), sk)
                    e = jnp.exp2(s - m)
                    den = jnp.sum(e, axis=0, keepdims=True) + jnp.exp2(sk - m)
                    ot = _dot_tn(vw, e.astype(BF16))
                    ot = ot[kvh * HEAD_DIM:(kvh + 1) * HEAD_DIM, :] * (1.0 / den)
                    oc = ot.T
                    oc = jnp.concatenate([oc[j * QS:(j + 1) * QS, :] for j in range(C_GROUP)], axis=1)
                    lo = A_WIDTH + B_WIDTH + kvh * C_GROUP * HEAD_DIM
                    o_ref[rows, lo:lo + C_GROUP * HEAD_DIM] = oc
            if carried:
                kwin[g, 0:WINDOW, :] = kwin[g, TB:TB + WINDOW, :]
                vwin[g, 0:WINDOW, :] = vwin[g, TB:TB + WINDOW, :]

    def new_states():
        kn = ok_ref.shape[1]
        for g in range(G):
            oconv_ref[g] = axbuf[g, SUBLANES + TB - 3:SUBLANES + TB, :]
            olru_ref[g] = hc_ref[g]
            for hd in range(B_HEADS):
                sl = slice(hd * HEAD_DIM, (hd + 1) * HEAD_DIM)
                oret_ref[g, hd] = sbd[g, sl, sl]
            last = slice((g + 1) * TB - kn, (g + 1) * TB)
            ok_ref[g] = z_ref[last, OFF_CK:OFF_CK + C_KV_WIDTH]
            ov_ref[g] = z_ref[last, OFF_CV:OFF_CV + C_KV_WIDTH]

    x = x_ref[...].reshape(R, D_MODEL)
    h = _ln(x, ln_in_g[...], ln_in_b[...]) if ln_in else x
    z_ref[...] = _dot(h.astype(BF16), w_in_ref[...])
    group_a()
    group_b()
    group_c()
    y = _dot(o_ref[...].astype(BF16), w_out_ref[...])
    h1_ref[...] = _ln(ALPHA * h + y, ln1_g[...], ln1_b[...]).reshape(G, TB, D_MODEL)
    pl.when(t_idx == NT - 1)(new_states)


def _const_spec(shape):
    nd = len(shape)
    return pl.BlockSpec(shape, lambda *_, _nd=nd: (0,) * _nd, pipeline_mode=pl.Buffered(1))


def _layer_spec(shape, layer):
    nd = len(shape) - 1
    return pl.BlockSpec((None,) + tuple(shape[1:]), lambda *_, _nd=nd: (layer,) + (0,) * _nd,
                        pipeline_mode=pl.Buffered(1))


def _mixer_call(x, states, layer, lp, tabs, *, G, TB, CR, QS, ln_in):
    B, T, _ = x.shape
    NS, NT = B // G, T // TB
    has_past = states is not None
    KN = TB if has_past else WINDOW
    cfg = (G, TB, CR, QS, NT, has_past, ln_in)

    in_specs = [pl.BlockSpec((G, TB, D_MODEL), lambda s, t: (s, t, 0))]
    args = [x]
    if has_past:
        for st in states:
            nd = st.ndim
            in_specs.append(pl.BlockSpec((None, G) + st.shape[2:],
                                         lambda s, t, _nd=nd: (layer, s) + (0,) * (_nd - 2)))
            args.append(st)
    for cst, stacked in lp:
        in_specs.append(_layer_spec(cst.shape, layer) if stacked else _const_spec(cst.shape))
        args.append(cst)
    for cst in tabs:
        in_specs.append(_const_spec(cst.shape))
        args.append(cst)

    out_shape = (
        jax.ShapeDtypeStruct((B, T, D_MODEL), F32),
        jax.ShapeDtypeStruct((B, CONV_W - 1, A_WIDTH), F32),
        jax.ShapeDtypeStruct((B, 1, A_WIDTH), F32),
        jax.ShapeDtypeStruct((B, B_HEADS, HEAD_DIM, HEAD_DIM), F32),
        jax.ShapeDtypeStruct((B, KN, C_KV_WIDTH), F32),
        jax.ShapeDtypeStruct((B, KN, C_KV_WIDTH), F32),
    )
    out_specs = (
        pl.BlockSpec((G, TB, D_MODEL), lambda s, t: (s, t, 0)),
        pl.BlockSpec((G, CONV_W - 1, A_WIDTH), lambda s, t: (s, 0, 0)),
        pl.BlockSpec((G, 1, A_WIDTH), lambda s, t: (s, 0, 0)),
        pl.BlockSpec((G, B_HEADS, HEAD_DIM, HEAD_DIM), lambda s, t: (s, 0, 0, 0)),
        pl.BlockSpec((G, KN, C_KV_WIDTH), lambda s, t: (s, 0, 0)),
        pl.BlockSpec((G, KN, C_KV_WIDTH), lambda s, t: (s, 0, 0)),
    )
    R = G * TB
    scratch = [
        pltpu.VMEM((R, D_IN), F32),
        pltpu.VMEM((R, D_MIX), F32),
        pltpu.VMEM((R, A_WIDTH), F32),
        pltpu.VMEM((G, SUBLANES + TB, A_WIDTH), F32),
        pltpu.VMEM((G, 1, A_WIDTH), F32),
        pltpu.VMEM((G, WINDOW + TB, C_KV_WIDTH), BF16),
        pltpu.VMEM((G, WINDOW + TB, C_KV_WIDTH), BF16),
        pltpu.VMEM((G, B_WIDTH, B_WIDTH), F32),
    ]
    return pl.pallas_call(
        functools.partial(_mixer_body, cfg),
        grid=(NS, NT),
        in_specs=in_specs,
        out_specs=out_specs,
        out_shape=out_shape,
        scratch_shapes=scratch,
        compiler_params=pltpu.CompilerParams(
            dimension_semantics=("arbitrary", "arbitrary"),
            vmem_limit_bytes=VMEM_LIMIT_BYTES),
        name="mixer_past" if has_past else "mixer_prompt",
    )(*args)


FFN_BOUNDS = (0, 6 * MXU_TILE, D_FF)
FFN_ROWS = 512


def _ffn_body(h_ref, wg_ref, wu_ref, wd_ref, g_ref, b_ref, out_ref):
    for r0 in range(0, h_ref.shape[0], FFN_ROWS):
        rows = slice(r0, r0 + FFN_ROWS)
        h = h_ref[rows, :]
        hb = h.astype(BF16)
        acc = jnp.zeros(h.shape, F32)
        for lo, hi in zip(FFN_BOUNDS[:-1], FFN_BOUNDS[1:]):
            cols = slice(lo, hi)
            a = _dot(hb, wg_ref[:, cols])
            b = _dot(hb, wu_ref[:, cols])
            act = (a * jax.nn.sigmoid(a) * b).astype(BF16)
            acc = acc + _dot(act, wd_ref[cols, :])
        out_ref[rows, :] = _ln(ALPHA * h + acc, g_ref[...], b_ref[...])


def _ffn_call(h2d, layer, wg, wu, wd, g, b, *, TM):
    N = h2d.shape[0]
    return pl.pallas_call(
        _ffn_body,
        grid=(N // TM,),
        in_specs=[pl.BlockSpec((TM, D_MODEL), lambda i: (i, 0))]
                 + [_layer_spec(w.shape, layer) for w in (wg, wu, wd, g, b)],
        out_specs=pl.BlockSpec((TM, D_MODEL), lambda i: (i, 0)),
        out_shape=jax.ShapeDtypeStruct((N, D_MODEL), F32),
        compiler_params=pltpu.CompilerParams(
            dimension_semantics=("arbitrary",),
            vmem_limit_bytes=VMEM_LIMIT_BYTES),
        name="ffn",
    )(h2d, wg, wu, wd, g, b)


def _block_diag(w):
    nl, hh = w.shape[0], w.shape[1]
    eye = jnp.eye(hh, dtype=w.dtype)
    return (eye[None, :, None, :, None] * w[:, :, :, None, :]).reshape(nl, hh * HEAD_DIM, hh * HEAD_DIM)


def kernel(x_prompt, x_sample, state_conv, state_lru, state_ret, cache_k, cache_v, ln_in_g, ln_in_b, w_in, conv_w, conv_b, w_rg_a, b_rg_a, w_rg_x, b_rg_x, lru_lambda, ret_gn_g, ret_gn_b, sinks, w_out, ln1_g, ln1_b, w_gate, w_up, w_down, ln2_g, ln2_b):
    nb_p, nb_s = x_prompt.shape[0], x_sample.shape[0]
    wc = cache_k.shape[2]
    rows = lambda v: v.reshape(DEPTH, 1, -1)
    w_in_b = w_in.astype(BF16)
    w_out_b = w_out.astype(BF16)
    w_gate_b = w_gate.astype(BF16)
    w_up_b = w_up.astype(BF16)
    w_down_b = w_down.astype(BF16)
    ln2_gr, ln2_br = rows(ln2_g), rows(ln2_b)

    PROMPT = dict(G=2, TB=512, CR=256, QS=128)
    SAMPLE = dict(G=16, TB=32, CR=32, QS=32)
    FFN_TM = 1024
    tabs_p = _retention_tables(PROMPT["CR"]) + (_alibi_table(PROMPT["QS"]),)
    tabs_s = _retention_tables(SAMPLE["CR"]) + (_alibi_table(SAMPLE["QS"]),)

    w_gates = jnp.concatenate([_block_diag(w_rg_a), _block_diag(w_rg_x)], axis=2).astype(BF16)
    b_gates = jnp.concatenate([b_rg_a, b_rg_x], axis=1).reshape(DEPTH, 1, 2 * A_WIDTH)
    shared = [(w_in_b, True), (conv_w, True), (rows(conv_b), True), (w_gates, True), (b_gates, True),
              (rows(lru_lambda), True), (rows(ret_gn_g), True), (rows(ret_gn_b), True)]
    tail = [(w_out_b, True), (rows(ln1_g), True), (rows(ln1_b), True)]
    head = [(ln_in_g.reshape(1, -1), False), (ln_in_b.reshape(1, -1), False)]

    def mixer_params(qs):
        sink_rows = jnp.repeat(sinks * LOG2E, qs, axis=1).reshape(DEPTH, C_KV_HEADS, 1, C_GROUP * qs)
        return head + shared + [(sink_rows, True)] + tail

    lp_p, lp_s = mixer_params(PROMPT["QS"]), mixer_params(SAMPLE["QS"])
    states = (state_conv, state_lru.reshape(DEPTH, nb_s, 1, A_WIDTH), state_ret,
              cache_k.reshape(DEPTH, nb_s, wc, C_KV_WIDTH), cache_v.reshape(DEPTH, nb_s, wc, C_KV_WIDTH))

    hp, hs = x_prompt, x_sample
    outs_p, outs_s = [], []
    for l in range(DEPTH):
        res = _mixer_call(hp, None, l, lp_p, tabs_p, ln_in=(l == 0), **PROMPT)
        hp, st = res[0], res[1:]
        outs_p.append(st)
        hp = _ffn_call(hp.reshape(-1, D_MODEL), l, w_gate_b, w_up_b, w_down_b, ln2_gr, ln2_br,
                       TM=FFN_TM).reshape(x_prompt.shape)
        res = _mixer_call(hs, states, l, lp_s, tabs_s, ln_in=(l == 0), **SAMPLE)
        hs, st = res[0], res[1:]
        outs_s.append(st)
        hs = _ffn_call(hs.reshape(-1, D_MODEL), l, w_gate_b, w_up_b, w_down_b, ln2_gr, ln2_br,
                       TM=FFN_TM).reshape(x_sample.shape)

    def stack(outs, nb, kn):
        conv = jnp.stack([o[0] for o in outs])
        lru = jnp.stack([o[1].reshape(nb, A_WIDTH) for o in outs])
        ret = jnp.stack([o[2] for o in outs])
        kk = jnp.stack([o[3].reshape(nb, kn, C_KV_HEADS, HEAD_DIM) for o in outs])
        vv = jnp.stack([o[4].reshape(nb, kn, C_KV_HEADS, HEAD_DIM) for o in outs])
        return conv, lru, ret, kk, vv

    p_conv, p_lru, p_ret, p_k, p_v = stack(outs_p, nb_p, WINDOW)
    s_conv, s_lru, s_ret, s_k, s_v = stack(outs_s, nb_s, SAMPLE["TB"])
    return (hp, hs, p_conv, p_lru, p_ret, p_k, p_v, s_conv, s_lru, s_ret, s_k, s_v)
```

```python
import functools
import math

import numpy as np
import jax
import jax.numpy as jnp
from jax import lax
from jax.experimental import pallas as pl
from jax.experimental.pallas import tpu as pltpu

F32 = jnp.float32
BF16 = jnp.bfloat16

D_MODEL = 1024
DEPTH = 4
HEAD_DIM = 64
A_WIDTH = 256
B_HEADS = 4
B_WIDTH = 256
C_HEADS = 8
C_KV_HEADS = 2
C_GROUP = 4
C_WIDTH = 512
C_KV_WIDTH = 128
CONV_W = 4
LRU_C = 8.0
WINDOW = 128
ATTN_CHUNK = 64
D_MIX = 1024
D_IN = 2304
D_FF = 2816
ALPHA = (2.0 * DEPTH) ** 0.25
LN_EPS = 1e-5
NEG = -1e30
LOG2E = math.log2(math.e)

OFF_AX, OFF_AG = 0, 256
OFF_BQ, OFF_BK, OFF_BV, OFF_BG = 512, 768, 1024, 1280
OFF_CQ, OFF_CK, OFF_CV = 1536, 2048, 2176

SUBLANES = 8
MXU_TILE = 256
VMEM_LIMIT_BYTES = 52 * 1024 * 1024


def _ln(x, g, b):
    mu = jnp.mean(x, axis=-1, keepdims=True)
    d = x - mu
    var = jnp.mean(d * d, axis=-1, keepdims=True)
    return d * lax.rsqrt(var + LN_EPS) * g + b


def _dot(a, b):
    return jnp.dot(a, b, preferred_element_type=F32)


def _dot_nt(a, b):
    return lax.dot_general(a, b, (((1,), (1,)), ((), ())), preferred_element_type=F32)


def _dot_tn(a, b):
    return lax.dot_general(a, b, (((0,), (0,)), ((), ())), preferred_element_type=F32)


def _retention_tables(chunk):
    lg = np.log1p(-(2.0 ** (-5.0 - np.arange(B_HEADS, dtype=np.float64))))
    idx = np.arange(chunk, dtype=np.float64)
    diff = idx[:, None] - idx[None, :]
    dmat = np.where(diff >= 0, np.exp(np.maximum(diff, 0.0)[None] * lg[:, None, None]), 0.0)
    dmat = dmat * HEAD_DIM ** -0.5
    dmat_all = np.concatenate([dmat[h] for h in range(B_HEADS)], axis=1)
    w_start = np.exp((idx + 1.0)[:, None] * lg[None, :])
    w_end = np.exp((chunk - 1.0 - idx)[:, None] * lg[None, :]) * HEAD_DIM ** -0.5
    w_start = np.repeat(w_start, HEAD_DIM, axis=1)
    w_end = np.repeat(w_end, HEAD_DIM, axis=1)
    g_chunk = np.repeat(np.exp(chunk * lg), HEAD_DIM)
    head = np.arange(B_WIDTH) // HEAD_DIM
    blk = (head[:, None] == head[None, :]).astype(np.float64)
    g_bd = blk * g_chunk[:, None]
    avg_bd = blk / HEAD_DIM
    return (jnp.asarray(dmat_all, F32), jnp.asarray(w_start, F32), jnp.asarray(w_end, F32),
            jnp.asarray(g_bd, F32), jnp.asarray(blk, F32), jnp.asarray(avg_bd, BF16))


def _alibi_table(qs):
    w = WINDOW + qs
    slopes = 2.0 ** (-8.0 * np.arange(1, C_HEADS + 1, dtype=np.float64) / C_HEADS)
    qpos = np.arange(qs)
    kpos = np.arange(w) - WINDOW
    dist = np.abs(qpos[:, None] - kpos[None, :]).astype(np.float64)
    q_chunk = qpos // ATTN_CHUNK
    k_chunk = np.floor_divide(kpos, ATTN_CHUNK)
    visible = (k_chunk[None, :] <= q_chunk[:, None]) & (k_chunk[None, :] >= q_chunk[:, None] - WINDOW // ATTN_CHUNK)
    tab = slopes[:, None, None] * dist[None]
    tab = np.where(visible[None], tab * LOG2E, -NEG)
    tab = tab.reshape(C_KV_HEADS, C_GROUP * qs, w).transpose(0, 2, 1)
    return jnp.asarray(tab, F32)


def _mixer_body(cfg, *refs):
    G, TB, CR, QS, NT, has_past, ln_in = cfg
    R = G * TB
    W = WINDOW + QS
    carried = NT > 1
    it = iter(refs)
    x_ref = next(it)
    if has_past:
        sconv_ref, slru_ref, sret_ref, ck_ref, cv_ref = (next(it) for _ in range(5))
    (ln_in_g, ln_in_b, w_in_ref, conv_w_ref, conv_b_ref, w_gate_ref, b_gate_ref, lam_ref,
     gn_g_ref, gn_b_ref, sink_ref, w_out_ref, ln1_g, ln1_b,
     dmat_ref, wstart_ref, wend_ref, gbd_ref, blk_ref, avg_ref, alibi_ref) = (next(it) for _ in range(21))
    h1_ref, oconv_ref, olru_ref, oret_ref, ok_ref, ov_ref = (next(it) for _ in range(6))
    z_ref, o_ref, sb_ref, axbuf, hc_ref, kwin, vwin, sbd = (next(it) for _ in range(8))

    t_idx = pl.program_id(1)

    @pl.when(t_idx == 0)
    def _init():
        for g in range(G):
            if has_past:
                axbuf[g, SUBLANES - 3:SUBLANES, :] = sconv_ref[g]
                hc_ref[g] = slru_ref[g]
                sbd[g] = jnp.zeros((B_WIDTH, B_WIDTH), F32)
                for hd in range(B_HEADS):
                    sl = slice(hd * HEAD_DIM, (hd + 1) * HEAD_DIM)
                    sbd[g, sl, sl] = sret_ref[g, hd]
                kwin[g, 0:WINDOW, :] = ck_ref[g].astype(BF16)
                vwin[g, 0:WINDOW, :] = cv_ref[g].astype(BF16)
            else:
                axbuf[g, 0:SUBLANES, :] = jnp.zeros((SUBLANES, A_WIDTH), F32)
                hc_ref[g] = jnp.zeros((1, A_WIDTH), F32)
                sbd[g] = jnp.zeros((B_WIDTH, B_WIDTH), F32)
                kwin[g, 0:WINDOW, :] = jnp.zeros((WINDOW, C_KV_WIDTH), BF16)
                vwin[g, 0:WINDOW, :] = jnp.zeros((WINDOW, C_KV_WIDTH), BF16)

    def group_a():
        nl = -lam_ref[...]
        softplus_nl = jnp.maximum(nl, 0.0) + jnp.log1p(jnp.exp(-jnp.abs(nl)))
        cw = conv_w_ref[...]
        row_in_tile = lax.broadcasted_iota(jnp.int32, (TB // SUBLANES, SUBLANES, A_WIDTH), 1)
        for g in range(G):
            rows = slice(g * TB, (g + 1) * TB)
            ax = z_ref[rows, OFF_AX:OFF_AX + A_WIDTH]
            axbuf[g, SUBLANES:SUBLANES + TB, :] = ax
            u = conv_b_ref[...] + axbuf[g, SUBLANES - 3:SUBLANES - 3 + TB, :] * cw[0:1, :]
            u = u + axbuf[g, SUBLANES - 2:SUBLANES - 2 + TB, :] * cw[1:2, :]
            u = u + axbuf[g, SUBLANES - 1:SUBLANES - 1 + TB, :] * cw[2:3, :]
            u = u + ax * cw[3:4, :]
            if carried:
                axbuf[g, SUBLANES - 3:SUBLANES, :] = axbuf[g, SUBLANES + TB - 3:SUBLANES + TB, :]
            gz = _dot(u.astype(BF16), w_gate_ref[...]) + b_gate_ref[...]
            r = jax.nn.sigmoid(gz[:, 0:A_WIDTH])
            gi = jax.nn.sigmoid(gz[:, A_WIDTH:2 * A_WIDTH])
            log_a = (-LRU_C) * r * softplus_nl
            a = jnp.exp(log_a)
            th = jnp.tanh(log_a)
            one_minus_a2 = (-2.0) * th / (1.0 - th)
            bt = jnp.sqrt(one_minus_a2) * (gi * u)
            a = a.reshape(TB // SUBLANES, SUBLANES, A_WIDTH)
            bt = bt.reshape(TB // SUBLANES, SUBLANES, A_WIDTH)
            for s in (1, 2, 4):
                keep = row_in_tile >= s
                bt = jnp.where(keep, a * pltpu.roll(bt, s, 1) + bt, bt)
                a = jnp.where(keep, a * pltpu.roll(a, s, 1), a)
            hprev = jnp.broadcast_to(hc_ref[g], (SUBLANES, A_WIDTH))
            for i in range(TB // SUBLANES):
                ht = a[i] * hprev + bt[i]
                sb_ref[g * TB + i * SUBLANES:g * TB + (i + 1) * SUBLANES, :] = ht
                hprev = jnp.broadcast_to(ht[SUBLANES - 1:SUBLANES, :], (SUBLANES, A_WIDTH))
            hc_ref[g] = hprev[0:1, :]
            ha = sb_ref[rows, :]
            ag = z_ref[rows, OFF_AG:OFF_AG + A_WIDTH]
            o_ref[rows, 0:A_WIDTH] = ha * jax.nn.gelu(ag)

    def group_b():
        lane_head = lax.broadcasted_iota(jnp.int32, (CR, B_WIDTH), 1) // HEAD_DIM
        for g in range(G):
            for c in range(TB // CR):
                rows = slice(g * TB + c * CR, g * TB + (c + 1) * CR)
                q = z_ref[rows, OFF_BQ:OFF_BQ + B_WIDTH]
                k = z_ref[rows, OFF_BK:OFF_BK + B_WIDTH]
                v = z_ref[rows, OFF_BV:OFF_BV + B_WIDTH]
                kx = jnp.concatenate([jnp.where(lane_head == hh, k, 0.0) for hh in range(B_HEADS)], axis=0).astype(BF16)
                vx = jnp.concatenate([jnp.where(lane_head == hh, v, 0.0) for hh in range(B_HEADS)], axis=0).astype(BF16)
                inner = _dot_nt(q.astype(BF16), kx) * dmat_ref[...]
                s_prev = sbd[g]
                ro = _dot(inner.astype(BF16), vx) + _dot((q * wstart_ref[...]).astype(BF16), s_prev.astype(BF16))
                upd = _dot_tn((k * wend_ref[...]).astype(BF16), v.astype(BF16))
                sbd[g] = gbd_ref[...] * s_prev + upd * blk_ref[...]
                o_ref[rows, A_WIDTH:A_WIDTH + B_WIDTH] = ro
        ro = o_ref[:, A_WIDTH:A_WIDTH + B_WIDTH]
        bgate = z_ref[:, OFF_BG:OFF_BG + B_WIDTH]
        mu = _dot(ro.astype(BF16), avg_ref[...])
        dlt = ro - mu
        var = _dot((dlt * dlt).astype(BF16), avg_ref[...])
        on = dlt * lax.rsqrt(var + LN_EPS) * gn_g_ref[...] + gn_b_ref[...]
        o_ref[:, A_WIDTH:A_WIDTH + B_WIDTH] = on * (bgate * jax.nn.sigmoid(bgate))

    def group_c():
        for g in range(G):
            rows_g = slice(g * TB, (g + 1) * TB)
            kwin[g, WINDOW:WINDOW + TB, :] = z_ref[rows_g, OFF_CK:OFF_CK + C_KV_WIDTH].astype(BF16)
            vwin[g, WINDOW:WINDOW + TB, :] = z_ref[rows_g, OFF_CV:OFF_CV + C_KV_WIDTH].astype(BF16)
            for c in range(TB // QS):
                rows = slice(g * TB + c * QS, g * TB + (c + 1) * QS)
                kw = kwin[g, c * QS:c * QS + W, :]
                vw = vwin[g, c * QS:c * QS + W, :]
                for kvh in range(C_KV_HEADS):
                    qs = jnp.concatenate(
                        [z_ref[rows, OFF_CQ + (kvh * C_GROUP + j) * HEAD_DIM:OFF_CQ + (kvh * C_GROUP + j + 1) * HEAD_DIM]
                         for j in range(C_GROUP)], axis=0)
                    qs = (qs * (HEAD_DIM ** -0.5 * LOG2E)).astype(BF16)
                    s = _dot_nt(kw[:, kvh * HEAD_DIM:(kvh + 1) * HEAD_DIM], qs) - alibi_ref[kvh]
                    if (not has_past) and c == 0:
                        key_id = lax.broadcasted_iota(jnp.int32, (W, C_GROUP * QS), 0)
                        n_invalid = jnp.where(t_idx == 0, WINDOW, 0)
                        s = jnp.where(key_id >= n_invalid, s, NEG)
                    sk = sink_ref[kvh]
                    m = jnp.maximum(jnp.max(s, axis=0, keepdims=True), sk)
                    e = jnp.exp2(s - m)
                    den = jnp.sum(e, axis=0, keepdims=True) + jnp.exp2(sk - m)
                    ot = _dot_tn(vw, e.astype(BF16))
                    ot = ot[kvh * HEAD_DIM:(kvh + 1) * HEAD_DIM, :] * (1.0 / den)
                    oc = ot.T
                    oc = jnp.concatenate([oc[j * QS:(j + 1) * QS, :] for j in range(C_GROUP)], axis=1)
                    lo = A_WIDTH + B_WIDTH + kvh * C_GROUP * HEAD_DIM
                    o_ref[rows, lo:lo + C_GROUP * HEAD_DIM] = oc
            if carried:
                kwin[g, 0:WINDOW, :] = kwin[g, TB:TB + WINDOW, :]
                vwin[g, 0:WINDOW, :] = vwin[g, TB:TB + WINDOW, :]

    def new_states():
        kn = ok_ref.shape[1]
        for g in range(G):
            oconv_ref[g] = axbuf[g, SUBLANES + TB - 3:SUBLANES + TB, :]
            olru_ref[g] = hc_ref[g]
            for hd in range(B_HEADS):
                sl = slice(hd * HEAD_DIM, (hd + 1) * HEAD_DIM)
                oret_ref[g, hd] = sbd[g, sl, sl]
            last = slice((g + 1) * TB - kn, (g + 1) * TB)
            ok_ref[g] = z_ref[last, OFF_CK:OFF_CK + C_KV_WIDTH]
            ov_ref[g] = z_ref[last, OFF_CV:OFF_CV + C_KV_WIDTH]

    x = x_ref[...].reshape(R, D_MODEL)
    h = _ln(x, ln_in_g[...], ln_in_b[...]) if ln_in else x
    z_ref[...] = _dot(h.astype(BF16), w_in_ref[...])
    group_a()
    group_b()
    group_c()
    y = _dot(o_ref[...].astype(BF16), w_out_ref[...])
    h1_ref[...] = _ln(ALPHA * h + y, ln1_g[...], ln1_b[...]).reshape(G, TB, D_MODEL)
    pl.when(t_idx == NT - 1)(new_states)


def _const_spec(shape):
    nd = len(shape)
    return pl.BlockSpec(shape, lambda *_, _nd=nd: (0,) * _nd, pipeline_mode=pl.Buffered(1))


def _layer_spec(shape, layer):
    nd = len(shape) - 1
    return pl.BlockSpec((None,) + tuple(shape[1:]), lambda *_, _nd=nd: (layer,) + (0,) * _nd,
                        pipeline_mode=pl.Buffered(1))


def _mixer_call(x, states, layer, lp, tabs, *, G, TB, CR, QS, ln_in):
    B, T, _ = x.shape
    NS, NT = B // G, T // TB
    has_past = states is not None
    KN = TB if has_past else WINDOW
    cfg = (G, TB, CR, QS, NT, has_past, ln_in)

    in_specs = [pl.BlockSpec((G, TB, D_MODEL), lambda s, t: (s, t, 0))]
    args = [x]
    if has_past:
        for st in states:
            nd = st.ndim
            in_specs.append(pl.BlockSpec((None, G) + st.shape[2:],
                                         lambda s, t, _nd=nd: (layer, s) + (0,) * (_nd - 2)))
            args.append(st)
    for cst, stacked in lp:
        in_specs.append(_layer_spec(cst.shape, layer) if stacked else _const_spec(cst.shape))
        args.append(cst)
    for cst in tabs:
        in_specs.append(_const_spec(cst.shape))
        args.append(cst)

    out_shape = (
        jax.ShapeDtypeStruct((B, T, D_MODEL), F32),
        jax.ShapeDtypeStruct((B, CONV_W - 1, A_WIDTH), F32),
        jax.ShapeDtypeStruct((B, 1, A_WIDTH), F32),
        jax.ShapeDtypeStruct((B, B_HEADS, HEAD_DIM, HEAD_DIM), F32),
        jax.ShapeDtypeStruct((B, KN, C_KV_WIDTH), F32),
        jax.ShapeDtypeStruct((B, KN, C_KV_WIDTH), F32),
    )
    out_specs = (
        pl.BlockSpec((G, TB, D_MODEL), lambda s, t: (s, t, 0)),
        pl.BlockSpec((G, CONV_W - 1, A_WIDTH), lambda s, t: (s, 0, 0)),
        pl.BlockSpec((G, 1, A_WIDTH), lambda s, t: (s, 0, 0)),
        pl.BlockSpec((G, B_HEADS, HEAD_DIM, HEAD_DIM), lambda s, t: (s, 0, 0, 0)),
        pl.BlockSpec((G, KN, C_KV_WIDTH), lambda s, t: (s, 0, 0)),
        pl.BlockSpec((G, KN, C_KV_WIDTH), lambda s, t: (s, 0, 0)),
    )
    R = G * TB
    scratch = [
        pltpu.VMEM((R, D_IN), F32),
        pltpu.VMEM((R, D_MIX), F32),
        pltpu.VMEM((R, A_WIDTH), F32),
        pltpu.VMEM((G, SUBLANES + TB, A_WIDTH), F32),
        pltpu.VMEM((G, 1, A_WIDTH), F32),
        pltpu.VMEM((G, WINDOW + TB, C_KV_WIDTH), BF16),
        pltpu.VMEM((G, WINDOW + TB, C_KV_WIDTH), BF16),
        pltpu.VMEM((G, B_WIDTH, B_WIDTH), F32),
    ]
    return pl.pallas_call(
        functools.partial(_mixer_body, cfg),
        grid=(NS, NT),
        in_specs=in_specs,
        out_specs=out_specs,
        out_shape=out_shape,
        scratch_shapes=scratch,
        compiler_params=pltpu.CompilerParams(
            dimension_semantics=("arbitrary", "arbitrary"),
            vmem_limit_bytes=VMEM_LIMIT_BYTES),
        name="mixer_past" if has_past else "mixer_prompt",
    )(*args)


FFN_BOUNDS = (0, 6 * MXU_TILE, D_FF)
FFN_ROWS = 512


def _ffn_body(n_first, ha_ref, hb_ref, wg_ref, wu_ref, wd_ref, g_ref, b_ref, outa_ref, outb_ref):
    def run(h_ref, out_ref):
        for r0 in range(0, h_ref.shape[0], FFN_ROWS):
            rows = slice(r0, r0 + FFN_ROWS)
            h = h_ref[rows, :]
            hb = h.astype(BF16)
            acc = jnp.zeros(h.shape, F32)
            for lo, hi in zip(FFN_BOUNDS[:-1], FFN_BOUNDS[1:]):
                cols = slice(lo, hi)
                a = _dot(hb, wg_ref[:, cols])
                b = _dot(hb, wu_ref[:, cols])
                act = (a * jax.nn.sigmoid(a) * b).astype(BF16)
                acc = acc + _dot(act, wd_ref[cols, :])
            out_ref[rows, :] = _ln(ALPHA * h + acc, g_ref[...], b_ref[...])

    step = pl.program_id(0)
    pl.when(step < n_first)(functools.partial(run, ha_ref, outa_ref))
    pl.when(step >= n_first)(functools.partial(run, hb_ref, outb_ref))


def _ffn_call(ha2d, hb2d, layer, wg, wu, wd, g, b, *, TM):
    na, nb = ha2d.shape[0] // TM, hb2d.shape[0] // TM
    first = lambda i: (jnp.minimum(i, na - 1), 0)
    second = lambda i: (jnp.maximum(i - na, 0), 0)
    return pl.pallas_call(
        functools.partial(_ffn_body, na),
        grid=(na + nb,),
        in_specs=[pl.BlockSpec((TM, D_MODEL), first), pl.BlockSpec((TM, D_MODEL), second)]
                 + [_layer_spec(w.shape, layer) for w in (wg, wu, wd, g, b)],
        out_specs=(pl.BlockSpec((TM, D_MODEL), first), pl.BlockSpec((TM, D_MODEL), second)),
        out_shape=(jax.ShapeDtypeStruct(ha2d.shape, F32), jax.ShapeDtypeStruct(hb2d.shape, F32)),
        compiler_params=pltpu.CompilerParams(
            dimension_semantics=("arbitrary",),
            vmem_limit_bytes=VMEM_LIMIT_BYTES),
        name="ffn",
    )(ha2d, hb2d, wg, wu, wd, g, b)


def _block_diag(w):
    nl, hh = w.shape[0], w.shape[1]
    eye = jnp.eye(hh, dtype=w.dtype)
    return (eye[None, :, None, :, None] * w[:, :, :, None, :]).reshape(nl, hh * HEAD_DIM, hh * HEAD_DIM)


def kernel(x_prompt, x_sample, state_conv, state_lru, state_ret, cache_k, cache_v, ln_in_g, ln_in_b, w_in, conv_w, conv_b, w_rg_a, b_rg_a, w_rg_x, b_rg_x, lru_lambda, ret_gn_g, ret_gn_b, sinks, w_out, ln1_g, ln1_b, w_gate, w_up, w_down, ln2_g, ln2_b):
    nb_p, nb_s = x_prompt.shape[0], x_sample.shape[0]
    wc = cache_k.shape[2]
    rows = lambda v: v.reshape(DEPTH, 1, -1)
    w_in_b = w_in.astype(BF16)
    w_out_b = w_out.astype(BF16)
    w_gate_b = w_gate.astype(BF16)
    w_up_b = w_up.astype(BF16)
    w_down_b = w_down.astype(BF16)
    ln2_gr, ln2_br = rows(ln2_g), rows(ln2_b)

    PROMPT = dict(G=2, TB=512, CR=256, QS=128)
    SAMPLE = dict(G=8, TB=32, CR=32, QS=32)
    FFN_TM = 512
    tabs_p = _retention_tables(PROMPT["CR"]) + (_alibi_table(PROMPT["QS"]),)
    tabs_s = _retention_tables(SAMPLE["CR"]) + (_alibi_table(SAMPLE["QS"]),)

    w_gates = jnp.concatenate([_block_diag(w_rg_a), _block_diag(w_rg_x)], axis=2).astype(BF16)
    b_gates = jnp.concatenate([b_rg_a, b_rg_x], axis=1).reshape(DEPTH, 1, 2 * A_WIDTH)
    shared = [(w_in_b, True), (conv_w, True), (rows(conv_b), True), (w_gates, True), (b_gates, True),
              (rows(lru_lambda), True), (rows(ret_gn_g), True), (rows(ret_gn_b), True)]
    tail = [(w_out_b, True), (rows(ln1_g), True), (rows(ln1_b), True)]
    head = [(ln_in_g.reshape(1, -1), False), (ln_in_b.reshape(1, -1), False)]

    def mixer_params(qs):
        sink_rows = jnp.repeat(sinks * LOG2E, qs, axis=1).reshape(DEPTH, C_KV_HEADS, 1, C_GROUP * qs)
        return head + shared + [(sink_rows, True)] + tail

    lp_p, lp_s = mixer_params(PROMPT["QS"]), mixer_params(SAMPLE["QS"])
    states = (state_conv, state_lru.reshape(DEPTH, nb_s, 1, A_WIDTH), state_ret,
              cache_k.reshape(DEPTH, nb_s, wc, C_KV_WIDTH), cache_v.reshape(DEPTH, nb_s, wc, C_KV_WIDTH))

    hp, hs = x_prompt, x_sample
    outs_p, outs_s = [], []
    for l in range(DEPTH):
        res = _mixer_call(hp, None, l, lp_p, tabs_p, ln_in=(l == 0), **PROMPT)
        hp, st = res[0], res[1:]
        outs_p.append(st)
        res = _mixer_call(hs, states, l, lp_s, tabs_s, ln_in=(l == 0), **SAMPLE)
        hs, st = res[0], res[1:]
        outs_s.append(st)
        hp, hs = _ffn_call(hp.reshape(-1, D_MODEL), hs.reshape(-1, D_MODEL), l, w_gate_b, w_up_b, w_down_b,
                           ln2_gr, ln2_br, TM=FFN_TM)
        hp, hs = hp.reshape(x_prompt.shape), hs.reshape(x_sample.shape)

    def stack(outs, nb, kn):
        conv = jnp.stack([o[0] for o in outs])
        lru = jnp.stack([o[1].reshape(nb, A_WIDTH) for o in outs])
        ret = jnp.stack([o[2] for o in outs])
        kk = jnp.stack([o[3].reshape(nb, kn, C_KV_HEADS, HEAD_DIM) for o in outs])
        vv = jnp.stack([o[4].reshape(nb, kn, C_KV_HEADS, HEAD_DIM) for o in outs])
        return conv, lru, ret, kk, vv

    p_conv, p_lru, p_ret, p_k, p_v = stack(outs_p, nb_p, WINDOW)
    s_conv, s_lru, s_ret, s_k, s_v = stack(outs_s, nb_s, SAMPLE["TB"])
    return (hp, hs, p_conv, p_lru, p_ret, p_k, p_v, s_conv, s_lru, s_ret, s_k, s_v)
```
